```python
import math
import jax, jax.numpy as jnp
from jax import lax
import numpy as np

D_MODEL = 1024
BATCH = 16
SEQ = 4096
DEPTH = 1
DEC_BATCH = 32
DEC_SEQ = 2048
PAST_LEN = 128

RET_HEADS = 4
RET_DK = 256
RET_DV = 512
RET_CHUNK = 128
MLA_HEADS = 8
MLA_NOPE = 128
MLA_ROPE = 64
MLA_V = 128
Q_LORA = 384
KV_LORA = 256
Q_BLOCK = 128
ROPE_THETA = 10000.0
N_EXPERTS = 32
TOP_K = 4
D_FF = 1024
SWIGLU_LIMIT = 7.0
SWIGLU_ALPHA = 1.702
MOE_BLOCK = 128
LN_EPS = 1e-5
RMS_EPS = 1e-6
DEEPNORM_ALPHA = (2 * DEPTH) ** 0.25
DEEPNORM_BETA = (8 * DEPTH) ** -0.25

RET_QK_W = RET_HEADS * RET_DK
RET_V_W = RET_HEADS * RET_DV
SPLITS = [RET_QK_W, RET_QK_W, RET_V_W, RET_V_W, Q_LORA, KV_LORA, MLA_ROPE, D_MODEL, D_MODEL]
D_IN = sum(SPLITS)
SPLIT_POINTS = [sum(SPLITS[:i + 1]) for i in range(len(SPLITS) - 1)]

kernel_name = 'hybrid_retention_mla_moe_encoder'


def layer_norm(x, g, b):
    xf = x.astype(jnp.float32)
    mu = jnp.mean(xf, -1, keepdims=True)
    var = jnp.mean(jnp.square(xf - mu), -1, keepdims=True)
    return ((xf - mu) * lax.rsqrt(var + LN_EPS) * g + b).astype(x.dtype)


def rms_norm(x, g):
    xf = x.astype(jnp.float32)
    return (xf * lax.rsqrt(jnp.mean(jnp.square(xf), -1, keepdims=True) + RMS_EPS) * g).astype(x.dtype)


def rope_tables(seq, dim):
    inv = ROPE_THETA ** (-jnp.arange(0, dim, 2, dtype=jnp.float32) / dim)
    ang = jnp.arange(seq, dtype=jnp.float32)[:, None] * inv[None, :]
    return jnp.cos(ang), jnp.sin(ang)


def apply_rope(x, cos, sin):
    x1, x2 = jnp.split(x, 2, axis=-1)
    c = cos[:, None, :]
    s = sin[:, None, :]
    return jnp.concatenate([x1 * c - x2 * s, x1 * s + x2 * c], axis=-1).astype(x.dtype)


def retention_scan(q, k, v, log_g, strict):
    B, H, S, dk = q.shape
    dv = v.shape[-1]
    C = RET_CHUNK
    NC = S // C
    idx = jnp.arange(C, dtype=jnp.float32)
    diff = idx[:, None] - idx[None, :]
    mask = (diff > 0) if strict else (diff >= 0)
    decay = jnp.where(mask, jnp.exp(log_g[:, None, None] * jnp.maximum(diff, 0.0)), 0.0)
    xi = jnp.exp(log_g[:, None] * (idx + 1.0))[..., None]
    zeta = jnp.exp(log_g[:, None] * (C - 1.0 - idx))[..., None]
    g_chunk = jnp.exp(log_g * C)[:, None, None]

    def to_chunks(t):
        return t.reshape(B, H, NC, C, t.shape[-1]).transpose(2, 0, 1, 3, 4)

    def step(R, inp):
        qi, ki, vi = inp
        inner = jnp.einsum('bhnd,bhmd->bhnm', qi, ki) * decay
        o = jnp.einsum('bhnm,bhmv->bhnv', inner, vi) + jnp.einsum('bhnd,bhdv->bhnv', qi * xi, R)
        R = R * g_chunk + jnp.einsum('bhmd,bhmv->bhdv', ki * zeta, vi)
        return R, o

    R0 = jnp.zeros((B, H, dk, dv), jnp.float32)
    _, o = lax.scan(step, R0, (to_chunks(q), to_chunks(k), to_chunks(v)))
    return o.transpose(1, 2, 0, 3, 4).reshape(B, H, S, dv)


def retention_branch(q_raw, k_raw, v_raw, g_raw, decay_logit, gn_g, w_ret_o):
    B, S, _ = q_raw.shape
    f32 = jnp.float32
    cos, sin = rope_tables(S, RET_DK)
    q = apply_rope(q_raw.reshape(B, S, RET_HEADS, RET_DK), cos, sin)
    k = apply_rope(k_raw.reshape(B, S, RET_HEADS, RET_DK), cos, sin)
    v = v_raw.reshape(B, S, RET_HEADS, RET_DV)
    q = q.astype(f32).transpose(0, 2, 1, 3) * (RET_DK ** -0.5)
    k = k.astype(f32).transpose(0, 2, 1, 3)
    v = v.astype(f32).transpose(0, 2, 1, 3)
    log_gamma = jax.nn.log_sigmoid(decay_logit.astype(f32))
    o_fwd = retention_scan(q, k, v, log_gamma[0], strict=False)
    o_bwd = retention_scan(q[:, :, ::-1], k[:, :, ::-1], v[:, :, ::-1], log_gamma[1], strict=True)[:, :, ::-1]
    o = (o_fwd + o_bwd).transpose(0, 2, 1, 3)
    mu = jnp.mean(o, -1, keepdims=True)
    var = jnp.mean(jnp.square(o - mu), -1, keepdims=True)
    o = ((o - mu) * lax.rsqrt(var + LN_EPS)).reshape(B, S, RET_V_W) * gn_g
    y = (jax.nn.silu(g_raw.astype(f32)) * o).astype(q_raw.dtype)
    return y @ w_ret_o


def mla_branch(cq_raw, ckv_raw, kpe_raw, q_norm_g, w_uq, kv_norm_g, w_uk, w_uv, w_mla_o):
    B, S, _ = cq_raw.shape
    cos, sin = rope_tables(S, MLA_ROPE)
    c_q = rms_norm(cq_raw, q_norm_g)
    q = (c_q @ w_uq).reshape(B, S, MLA_HEADS, MLA_NOPE + MLA_ROPE)
    q_nope, q_pe = jnp.split(q, [MLA_NOPE], axis=-1)
    q_pe = apply_rope(q_pe, cos, sin)
    c_kv = rms_norm(ckv_raw, kv_norm_g)
    k_pe = apply_rope(kpe_raw[:, :, None, :], cos, sin)[:, :, 0]
    q_lat = jnp.einsum('bshn,chn->bshc', q_nope, w_uk)
    scale = (MLA_NOPE + MLA_ROPE) ** -0.5
    NB = S // Q_BLOCK

    def attend_block(args):
        ql, qp = args
        s = jnp.einsum('bqhc,bkc->bhqk', ql, c_kv) + jnp.einsum('bqhr,bkr->bhqk', qp, k_pe)
        p = jax.nn.softmax(s.astype(jnp.float32) * scale, axis=-1)
        return jnp.einsum('bhqk,bkc->bqhc', p.astype(c_kv.dtype), c_kv)

    ql_b = q_lat.reshape(B, NB, Q_BLOCK, MLA_HEADS, KV_LORA).transpose(1, 0, 2, 3, 4)
    qp_b = q_pe.reshape(B, NB, Q_BLOCK, MLA_HEADS, MLA_ROPE).transpose(1, 0, 2, 3, 4)
    o_lat = lax.map(attend_block, (ql_b, qp_b))
    o_lat = o_lat.transpose(1, 0, 2, 3, 4).reshape(B, S, MLA_HEADS, KV_LORA)
    o = jnp.einsum('bshc,chv->bshv', o_lat, w_uv).reshape(B, S, MLA_HEADS * MLA_V)
    return o @ w_mla_o


def moe_ffn(x, router_w, router_b, w_gate, b_gate, w_up, b_up, w_down, b_down):
    B, S, D = x.shape
    T = B * S
    xt = x.reshape(T, D)
    logits = (xt @ router_w).astype(jnp.float32) + router_b
    top_v, top_e = lax.top_k(logits, TOP_K)
    top_w = jax.nn.softmax(top_v, axis=-1)
    e_flat = top_e.reshape(-1).astype(jnp.int32)
    tok_flat = jnp.repeat(jnp.arange(T, dtype=jnp.int32), TOP_K)
    w_flat = top_w.reshape(-1)
    order = jnp.argsort(e_flat)
    e_s = e_flat[order]
    tok_s = tok_flat[order]
    w_s = w_flat[order]
    counts = jnp.zeros((N_EXPERTS,), jnp.int32).at[e_flat].add(1)
    padded = (counts + MOE_BLOCK - 1) // MOE_BLOCK * MOE_BLOCK
    start = jnp.cumsum(counts) - counts
    pend = jnp.cumsum(padded)
    pstart = pend - padded
    rank = jnp.arange(T * TOP_K, dtype=jnp.int32) - start[e_s]
    dest = pstart[e_s] + rank
    P = T * TOP_K + N_EXPERTS * MOE_BLOCK
    NBLK = P // MOE_BLOCK
    buf_tok = jnp.full((P,), T, jnp.int32).at[dest].set(tok_s)
    buf_w = jnp.zeros((P,), jnp.float32).at[dest].set(w_s)
    blk_start = jnp.arange(NBLK, dtype=jnp.int32) * MOE_BLOCK
    blk_e = jnp.minimum(jnp.searchsorted(pend, blk_start, side='right'), N_EXPERTS - 1).astype(jnp.int32)
    x_pad = jnp.concatenate([xt, jnp.zeros((1, D), xt.dtype)], axis=0)

    def step(out, inp):
        tok, w, e = inp
        xb = x_pad[tok]
        gate = jnp.minimum(xb @ w_gate[e] + b_gate[e], SWIGLU_LIMIT)
        up = jnp.clip(xb @ w_up[e] + b_up[e], -SWIGLU_LIMIT, SWIGLU_LIMIT)
        h = (up + 1.0) * (gate * jax.nn.sigmoid(SWIGLU_ALPHA * gate))
        y = h @ w_down[e] + b_down[e]
        return out.at[tok].add((y * w[:, None]).astype(out.dtype)), None

    out, _ = lax.scan(step, jnp.zeros((T + 1, D), x.dtype),
                      (buf_tok.reshape(NBLK, MOE_BLOCK), buf_w.reshape(NBLK, MOE_BLOCK), blk_e))
    return out[:T].reshape(B, S, D)


def encoder_layer(x, w_in, ret_decay_logit, ret_gn_g, w_ret_o, mla_q_norm_g, w_uq, mla_kv_norm_g,
                  w_uk, w_uv, w_mla_o, w_out, ln1_g, ln1_b, router_w, router_b, exp_w_gate, exp_b_gate,
                  exp_w_up, exp_b_up, exp_w_down, exp_b_down, ln2_g, ln2_b):
    proj = x @ w_in
    rq, rk, rv, rg, cq, ckv, kpe, gate_a, gate_b = jnp.split(proj, SPLIT_POINTS, axis=-1)
    y_a = retention_branch(rq, rk, rv, rg, ret_decay_logit, ret_gn_g, w_ret_o)
    y_b = mla_branch(cq, ckv, kpe, mla_q_norm_g, w_uq, mla_kv_norm_g, w_uk, w_uv, w_mla_o)
    merged = jax.nn.sigmoid(gate_a) * y_a + jax.nn.sigmoid(gate_b) * y_b
    h = layer_norm(DEEPNORM_ALPHA * x + merged @ w_out, ln1_g, ln1_b)
    f = moe_ffn(h, router_w, router_b, exp_w_gate, exp_b_gate, exp_w_up, exp_b_up, exp_w_down, exp_b_down)
    return layer_norm(DEEPNORM_ALPHA * h + f, ln2_g, ln2_b)


def setup_inputs(seed: int = 0) -> dict:
    key = jax.random.key(seed)
    ks = jax.random.split(key, 32)
    f32 = jnp.float32
    L, D = DEPTH, D_MODEL
    beta = DEEPNORM_BETA

    def nrm(k, shape, scale):
        return jax.random.normal(k, shape, f32) * scale

    s_in = D ** -0.5
    w_in = jnp.concatenate([
        nrm(ks[2], (L, D, RET_QK_W), s_in),
        nrm(ks[3], (L, D, RET_QK_W), s_in),
        nrm(ks[4], (L, D, RET_V_W), s_in * beta),
        nrm(ks[5], (L, D, RET_V_W), s_in),
        nrm(ks[6], (L, D, Q_LORA + KV_LORA + MLA_ROPE + 2 * D), s_in),
    ], axis=-1)
    base_gamma = 1.0 - 2.0 ** (-5.0 - np.arange(RET_HEADS))
    base_logit = jnp.asarray(np.log(base_gamma / (1.0 - base_gamma)), f32)
    return {
        'x_prompt': nrm(ks[0], (BATCH, SEQ, D), 1.0),
        'x_sample': nrm(ks[1], (DEC_BATCH, DEC_SEQ, D), 1.0),
        'w_in': w_in,
        'ret_decay_logit': base_logit + nrm(ks[7], (L, 2, RET_HEADS), 0.1),
        'ret_gn_g': 1.0 + nrm(ks[8], (L, RET_V_W), 0.02),
        'w_ret_o': nrm(ks[9], (L, RET_V_W, D), RET_V_W ** -0.5 * beta),
        'mla_q_norm_g': 1.0 + nrm(ks[10], (L, Q_LORA), 0.02),
        'w_uq': nrm(ks[11], (L, Q_LORA, MLA_HEADS * (MLA_NOPE + MLA_ROPE)), Q_LORA ** -0.5),
        'mla_kv_norm_g': 1.0 + nrm(ks[12], (L, KV_LORA), 0.02),
        'w_uk': nrm(ks[13], (L, KV_LORA, MLA_HEADS, MLA_NOPE), KV_LORA ** -0.5),
        'w_uv': nrm(ks[14], (L, KV_LORA, MLA_HEADS, MLA_V), KV_LORA ** -0.5 * beta),
        'w_mla_o': nrm(ks[15], (L, MLA_HEADS * MLA_V, D), (MLA_HEADS * MLA_V) ** -0.5 * beta),
        'w_out': nrm(ks[16], (L, D, D), s_in * beta),
        'ln1_g': 1.0 + nrm(ks[17], (L, D), 0.02),
        'ln1_b': nrm(ks[18], (L, D), 0.02),
        'router_w': nrm(ks[19], (L, D, N_EXPERTS), s_in),
        'router_b': nrm(ks[20], (L, N_EXPERTS), 0.01),
        'exp_w_gate': nrm(ks[21], (L, N_EXPERTS, D, D_FF), s_in),
        'exp_b_gate': nrm(ks[22], (L, N_EXPERTS, D_FF), 0.02),
        'exp_w_up': nrm(ks[23], (L, N_EXPERTS, D, D_FF), s_in),
        'exp_b_up': nrm(ks[24], (L, N_EXPERTS, D_FF), 0.02),
        'exp_w_down': nrm(ks[25], (L, N_EXPERTS, D_FF, D), D_FF ** -0.5 * beta),
        'exp_b_down': nrm(ks[26], (L, N_EXPERTS, D), 0.02),
        'ln2_g': 1.0 + nrm(ks[27], (L, D), 0.02),
        'ln2_b': nrm(ks[28], (L, D), 0.02),
    }


def reference(x_prompt, x_sample, w_in, ret_decay_logit, ret_gn_g, w_ret_o, mla_q_norm_g, w_uq,
              mla_kv_norm_g, w_uk, w_uv, w_mla_o, w_out, ln1_g, ln1_b, router_w, router_b,
              exp_w_gate, exp_b_gate, exp_w_up, exp_b_up, exp_w_down, exp_b_down, ln2_g, ln2_b):
    params = (w_in, ret_decay_logit, ret_gn_g, w_ret_o, mla_q_norm_g, w_uq, mla_kv_norm_g,
              w_uk, w_uv, w_mla_o, w_out, ln1_g, ln1_b, router_w, router_b,
              exp_w_gate, exp_b_gate, exp_w_up, exp_b_up, exp_w_down, exp_b_down, ln2_g, ln2_b)
    y_prompt = x_prompt
    y_sample = x_sample
    for l in range(DEPTH):
        layer_params = [p[l] for p in params]
        y_prompt = encoder_layer(y_prompt, *layer_params)
        y_sample = encoder_layer(y_sample, *layer_params)
    return (y_prompt, y_sample)
```

```python
import functools

import jax
import jax.numpy as jnp
from jax import lax
from jax.experimental import pallas as pl
from jax.experimental.pallas import tpu as pltpu

F32 = jnp.float32
BF16 = jnp.bfloat16
U32 = jnp.uint32
I32 = jnp.int32

D_MODEL = 1024
RET_HEADS = 4
RET_DK = 256
RET_DV = 512
RET_CHUNK = 128
MLA_HEADS = 8
MLA_NOPE = 128
MLA_ROPE = 64
MLA_V = 128
Q_LORA = 384
KV_LORA = 256
ROPE_THETA = 10000.0
N_EXPERTS = 32
TOP_K = 4
D_FF = 1024
SWIGLU_LIMIT = 7.0
SWIGLU_ALPHA = 1.702
LN_EPS = 1e-5
RMS_EPS = 1e-6
DEEPNORM_ALPHA = 2.0 ** 0.25
RET_QK_W = RET_HEADS * RET_DK
RET_V_W = RET_HEADS * RET_DV
MLA_SEG = Q_LORA + KV_LORA + 2 * MLA_ROPE
MLA_QK = 256
LOG2E = 1.4426950408889634

V7X_VMEM_LIMIT = 56 * 1024 * 1024
EXPERT_BLOCK = 512
HALF = D_MODEL // 2


def _params(*sem):
    return pltpu.CompilerParams(dimension_semantics=sem, vmem_limit_bytes=V7X_VMEM_LIMIT)


def _nt_dot(a, b):
    return lax.dot_general(a, b, (((1,), (1,)), ((), ())), preferred_element_type=F32)


def _tn_dot(a, b):
    return lax.dot_general(a, b, (((0,), (0,)), ((), ())), preferred_element_type=F32)


def _dot(a, b):
    return jnp.dot(a, b, preferred_element_type=F32)


def _pack_bf16_pair(x):
    bits = lax.bitcast_convert_type(x.astype(BF16).astype(F32), U32)
    return (bits[:, :HALF] >> 16) | bits[:, HALF:]


def _unpack_bf16_pair(p):
    lo = lax.bitcast_convert_type(p << 16, F32)
    hi = lax.bitcast_convert_type(p & jnp.uint32(0xFFFF0000), F32)
    return lo, hi


def _linear_kernel(x_ref, w_ref, o_ref, xb_ref):
    @pl.when(pl.program_id(1) == 0)
    def _():
        xb_ref[...] = x_ref[...].astype(BF16)

    o_ref[...] = _dot(xb_ref[...], w_ref[...]).astype(o_ref.dtype)


def _linear(x, w, out_dtype, tm, tn, name):
    M, K = x.shape
    N = w.shape[1]
    tm, tn = min(tm, M), min(tn, N)
    return pl.pallas_call(
        _linear_kernel,
        grid=(M // tm, N // tn),
        in_specs=[pl.BlockSpec((tm, K), lambda i, j: (i, 0)),
                  pl.BlockSpec((K, tn), lambda i, j: (0, j))],
        out_specs=pl.BlockSpec((tm, tn), lambda i, j: (i, j)),
        out_shape=jax.ShapeDtypeStruct((M, N), out_dtype),
        scratch_shapes=[pltpu.VMEM((tm, K), BF16)],
        compiler_params=_params("parallel", "arbitrary"),
        name=name,
    )(x, w)


def _qk_rope_kernel(x_ref, w_ref, cos_ref, sin_ref, o_ref, xb_ref, *, q_scale):
    j = pl.program_id(1)

    @pl.when(j == 0)
    def _():
        xb_ref[...] = x_ref[...].astype(BF16)

    y = _dot(xb_ref[...], w_ref[...])
    half = RET_DK // 2
    x1, x2 = y[:, :half], y[:, half:]
    c, s = cos_ref[...], sin_ref[...]
    scale = jnp.where(j < RET_HEADS, q_scale, 1.0).astype(F32)
    o_ref[:, :half] = ((x1 * c - x2 * s) * scale).astype(o_ref.dtype)
    o_ref[:, half:] = ((x1 * s + x2 * c) * scale).astype(o_ref.dtype)


def _qk_rope_proj(x, w_qk, cos, sin, seq, tm):
    M, K = x.shape
    N = w_qk.shape[1]
    tm = min(tm, seq)
    nseq = seq // tm
    return pl.pallas_call(
        functools.partial(_qk_rope_kernel, q_scale=RET_DK ** -0.5),
        grid=(M // tm, N // RET_DK),
        in_specs=[pl.BlockSpec((tm, K), lambda i, j: (i, 0)),
                  pl.BlockSpec((K, RET_DK), lambda i, j: (0, j)),
                  pl.BlockSpec((tm, RET_DK // 2), lambda i, j: (i % nseq, 0)),
                  pl.BlockSpec((tm, RET_DK // 2), lambda i, j: (i % nseq, 0))],
        out_specs=pl.BlockSpec((tm, RET_DK), lambda i, j: (i, j)),
        out_shape=jax.ShapeDtypeStruct((M, N), BF16),
        scratch_shapes=[pltpu.VMEM((tm, K), BF16)],
        compiler_params=_params("parallel", "arbitrary"),
        name="qk_rope_proj",
    )(x, w_qk, cos, sin)


def _retention_kernel(lg_ref, q_ref, k_ref, v_ref, g_ref, gn_ref, o_ref, ob_ref, r_ref):
    h = pl.program_id(1)
    S = q_ref.shape[1]
    C = RET_CHUNK
    NC = S // C
    lgf = lg_ref[0, h]
    lgb = lg_ref[1, h]

    n_col = lax.broadcasted_iota(I32, (C, 1), 0).astype(F32)
    n_mat = lax.broadcasted_iota(I32, (C, C), 0).astype(F32)
    m_mat = lax.broadcasted_iota(I32, (C, C), 1).astype(F32)
    diff = n_mat - m_mat
    decay = jnp.where(diff >= 0, jnp.exp(lgf * jnp.maximum(diff, 0.0)),
                      jnp.exp(lgb * jnp.maximum(-diff, 0.0)))
    xi_f = jnp.exp(lgf * (n_col + 1.0))
    zeta_f = jnp.exp(lgf * (C - 1.0 - n_col))
    xi_b = jnp.exp(lgb * (C - n_col))
    zeta_b = jnp.exp(lgb * n_col)
    ones = jnp.ones((1, 1), F32)
    g_f = jnp.exp(ones * (lgf * C))
    g_b = jnp.exp(ones * (lgb * C))

    def rows(i):
        return pl.ds(pl.multiple_of(i * C, C), C)

    r_ref[...] = jnp.zeros_like(r_ref)

    def bwd(jj, carry):
        r = rows(NC - 1 - jj)
        q = q_ref[0, r, :].astype(F32)
        k = k_ref[0, r, :].astype(F32)
        v = v_ref[0, r, :]
        state = r_ref[...]
        ob_ref[r, :] = _dot((q * xi_b).astype(BF16), state.astype(BF16))
        r_ref[...] = state * g_b + _tn_dot((k * zeta_b).astype(BF16), v)
        return carry

    lax.fori_loop(0, NC, bwd, 0)

    r_ref[...] = jnp.zeros_like(r_ref)
    gn = gn_ref[...]

    def fwd(i, carry):
        r = rows(i)
        qb = q_ref[0, r, :]
        kb = k_ref[0, r, :]
        q = qb.astype(F32)
        k = kb.astype(F32)
        v = v_ref[0, r, :]
        state = r_ref[...]
        inner = _nt_dot(qb, kb) * decay
        o = (_dot(inner.astype(BF16), v)
             + _dot((q * xi_f).astype(BF16), state.astype(BF16))
             + ob_ref[r, :])
        r_ref[...] = state * g_f + _tn_dot((k * zeta_f).astype(BF16), v)
        mu = jnp.mean(o, -1, keepdims=True)
        d = o - mu
        var = jnp.mean(d * d, -1, keepdims=True)
        on = d * lax.rsqrt(var + LN_EPS) * gn
        gate = g_ref[0, r, :].astype(F32)
        o_ref[0, r, :] = (gate * jax.nn.sigmoid(gate) * on).astype(o_ref.dtype)
        return carry

    lax.fori_loop(0, NC, fwd, 0)


def _retention(log_gamma, qk, vg, gn_g, B, S):
    qk3 = qk.reshape(B, S, 2 * RET_QK_W)
    vg3 = vg.reshape(B, S, vg.shape[1])
    H = RET_HEADS
    return pl.pallas_call(
        _retention_kernel,
        grid=(B, H),
        in_specs=[pl.BlockSpec(memory_space=pltpu.SMEM),
                  pl.BlockSpec((1, S, RET_DK), lambda b, h: (b, 0, h)),
                  pl.BlockSpec((1, S, RET_DK), lambda b, h: (b, 0, H + h)),
                  pl.BlockSpec((1, S, RET_DV), lambda b, h: (b, 0, h)),
                  pl.BlockSpec((1, S, RET_DV), lambda b, h: (b, 0, H + h)),
                  pl.BlockSpec((1, RET_DV), lambda b, h: (0, h))],
        out_specs=pl.BlockSpec((1, S, RET_DV), lambda b, h: (b, 0, h)),
        out_shape=jax.ShapeDtypeStruct((B, S, RET_V_W), BF16),
        scratch_shapes=[pltpu.VMEM((S, RET_DV), F32), pltpu.VMEM((RET_DK, RET_DV), F32)],
        compiler_params=_params("parallel", "arbitrary"),
        name="retention",
    )(log_gamma, qk3, qk3, vg3, vg3, gn_g)


def _rms(x, g):
    return x * lax.rsqrt(jnp.mean(x * x, -1, keepdims=True) + RMS_EPS) * g


def _mla_prep_kernel(m_ref, cc_ref, ss_ref, qg_ref, kg_ref, wq_ref, wk_ref, wv_ref,
                     q_ref, k_ref, v_ref, *, q_scale):
    x = m_ref[...]
    cc, ss = cc_ref[...], ss_ref[...]

    def rope(pair):
        return pair * cc + pltpu.roll(pair, MLA_ROPE, 1) * ss

    c_q = _rms(x[:, :Q_LORA], qg_ref[...]).astype(BF16)
    c_kv = _rms(x[:, Q_LORA:Q_LORA + KV_LORA], kg_ref[...]).astype(BF16)
    k_pe = rope(x[:, Q_LORA + KV_LORA:]).astype(BF16)
    q_all = _dot(c_q, wq_ref[...])
    k_all = _dot(c_kv, wk_ref[...])
    v_all = _dot(c_kv, wv_ref[...])
    for h in range(MLA_HEADS):
        qh = q_all[:, h * MLA_QK:(h + 1) * MLA_QK]
        q_ref[0, h, :, :MLA_NOPE] = (qh[:, :MLA_NOPE] * q_scale).astype(BF16)
        q_ref[0, h, :, MLA_NOPE:] = (rope(qh[:, MLA_NOPE:]) * q_scale).astype(BF16)
        k_ref[0, h, :, :MLA_NOPE] = k_all[:, h * MLA_NOPE:(h + 1) * MLA_NOPE].astype(BF16)
        k_ref[0, h, :, MLA_NOPE:] = k_pe
        v_ref[0, h, :, :] = v_all[:, h * MLA_V:(h + 1) * MLA_V].astype(BF16)


def _mla_prep(mla, cc, ss, qn_g, kvn_g, wq_ext, wk2, wv2, B, S, tm):
    tm = min(tm, S)
    ns = S // tm
    H = MLA_HEADS
    q_scale = (MLA_NOPE + MLA_ROPE) ** -0.5 * LOG2E
    full = lambda shape: pl.BlockSpec(shape, lambda b, i: (0,) * len(shape))
    return pl.pallas_call(
        functools.partial(_mla_prep_kernel, q_scale=q_scale),
        grid=(B, ns),
        in_specs=[pl.BlockSpec((tm, MLA_SEG), lambda b, i: (b * ns + i, 0)),
                  pl.BlockSpec((tm, 128), lambda b, i: (i, 0)),
                  pl.BlockSpec((tm, 128), lambda b, i: (i, 0)),
                  full((1, Q_LORA)), full((1, KV_LORA)),
                  full(wq_ext.shape), full(wk2.shape), full(wv2.shape)],
        out_specs=[pl.BlockSpec((1, H, tm, MLA_QK), lambda b, i: (b, 0, i, 0)),
                   pl.BlockSpec((1, H, tm, MLA_QK), lambda b, i: (b, 0, i, 0)),
                   pl.BlockSpec((1, H, tm, MLA_V), lambda b, i: (b, 0, i, 0))],
        out_shape=[jax.ShapeDtypeStruct((B, H, S, MLA_QK), BF16),
                   jax.ShapeDtypeStruct((B, H, S, MLA_QK), BF16),
                   jax.ShapeDtypeStruct((B, H, S, MLA_V), BF16)],
        compiler_params=_params("parallel", "parallel"),
        name="mla_prep",
    )(mla, cc, ss, qn_g, kvn_g, wq_ext, wk2, wv2)


def _attention_kernel(q_ref, k_ref, v_ref, o_ref):
    s = _nt_dot(q_ref[0, 0], k_ref[0, 0])
    m = jnp.max(s, -1, keepdims=True)
    p = jnp.exp2(s - m)
    l = jnp.sum(p, -1, keepdims=True)
    o = _dot(p.astype(BF16), v_ref[0, 0])
    o_ref[0] = (o / l).astype(o_ref.dtype)


def _attention(q, k, v, tq):
    B, H, S, _ = q.shape
    tq = min(tq, S)
    return pl.pallas_call(
        _attention_kernel,
        grid=(B, H, S // tq),
        in_specs=[pl.BlockSpec((1, 1, tq, MLA_QK), lambda b, h, i: (b, h, i, 0)),
                  pl.BlockSpec((1, 1, S, MLA_QK), lambda b, h, i: (b, h, 0, 0)),
                  pl.BlockSpec((1, 1, S, MLA_V), lambda b, h, i: (b, h, 0, 0))],
        out_specs=pl.BlockSpec((1, tq, MLA_V), lambda b, h, i: (b, i, h)),
        out_shape=jax.ShapeDtypeStruct((B, S, H * MLA_V), BF16),
        compiler_params=_params("parallel", "parallel", "arbitrary"),
        name="attention",
    )(q, k, v)


def _layer_norm(x, g, b):
    mu = jnp.mean(x, -1, keepdims=True)
    d = x - mu
    var = jnp.mean(d * d, -1, keepdims=True)
    return d * lax.rsqrt(var + LN_EPS) * g + b


def _merge_kernel(x_ref, yr_ref, om_ref, gate_ref, wro_ref, wmo_ref, wout_ref, g1_ref, b1_ref,
                  rwh_ref, rwl_ref, rb_ref, h_ref, hp_ref, te_ref, tw_ref):
    y_a = _dot(yr_ref[...], wro_ref[...])
    y_b = _dot(om_ref[...], wmo_ref[...])
    gates = gate_ref[...].astype(F32)
    merged = (jax.nn.sigmoid(gates[:, :D_MODEL]) * y_a + jax.nn.sigmoid(gates[:, D_MODEL:]) * y_b)
    mix = _dot(merged.astype(BF16), wout_ref[...])
    h = _layer_norm(DEEPNORM_ALPHA * x_ref[...] + mix, g1_ref[...], b1_ref[...])
    h_ref[...] = h
    hp_ref[...] = _pack_bf16_pair(h)

    h_hi = h.astype(BF16)
    h_lo = (h - h_hi.astype(F32)).astype(BF16)
    logits = (_dot(h_hi, rwh_ref[...]) + _dot(h_lo, rwh_ref[...]) + _dot(h_hi, rwl_ref[...])
              + rb_ref[...])
    lane = lax.broadcasted_iota(I32, logits.shape, 1)
    vals, idxs = [], []
    for _ in range(TOP_K):
        m = jnp.max(logits, -1, keepdims=True)
        idx = jnp.min(jnp.where(logits == m, lane, N_EXPERTS), -1, keepdims=True)
        vals.append(m)
        idxs.append(idx)
        logits = jnp.where(lane == idx, -jnp.inf, logits)
    e = [jnp.exp(v - vals[0]) for v in vals]
    tot = e[0] + e[1] + e[2] + e[3]
    te_ref[...] = jnp.concatenate(idxs, -1)
    tw_ref[...] = jnp.concatenate([ek / tot for ek in e], -1)


def _merge(x, y_ret, o_mla, vg, w_ret_o, w_mla_o, w_out, ln_g, ln_b, rw_hi, rw_lo, rb, tm):
    T = x.shape[0]
    tm = min(tm, T)
    gate_blk = (2 * RET_V_W) // (2 * D_MODEL)
    full = lambda a: pl.BlockSpec(a.shape, lambda i: (0,) * a.ndim)
    row = lambda w: pl.BlockSpec((tm, w), lambda i: (i, 0))
    return pl.pallas_call(
        _merge_kernel,
        grid=(T // tm,),
        in_specs=[row(D_MODEL), row(RET_V_W), row(D_MODEL),
                  pl.BlockSpec((tm, 2 * D_MODEL), lambda i: (i, gate_blk)),
                  full(w_ret_o), full(w_mla_o), full(w_out), full(ln_g), full(ln_b),
                  full(rw_hi), full(rw_lo), full(rb)],
        out_specs=[row(D_MODEL), row(HALF), row(TOP_K), row(TOP_K)],
        out_shape=[jax.ShapeDtypeStruct((T, D_MODEL), F32),
                   jax.ShapeDtypeStruct((T, HALF), U32),
                   jax.ShapeDtypeStruct((T, TOP_K), I32),
                   jax.ShapeDtypeStruct((T, TOP_K), F32)],
        compiler_params=_params("parallel"),
        name="merge_ln1_router",
    )(x, y_ret, o_mla, vg, w_ret_o, w_mla_o, w_out, ln_g, ln_b, rw_hi, rw_lo, rb)


def _rank_kernel(te_ref, rank_ref, cnt_ref, acc_ref):
    @pl.when(pl.program_id(0) == 0)
    def _():
        acc_ref[...] = jnp.zeros_like(acc_ref)

    tm = te_ref.shape[0]
    te = te_ref[...]
    lane = lax.broadcasted_iota(I32, (tm, N_EXPERTS), 1)
    earlier = (lax.broadcasted_iota(I32, (tm, tm), 0) > lax.broadcasted_iota(I32, (tm, tm), 1))
    earlier = jnp.where(earlier, 1.0, 0.0).astype(BF16)
    base = acc_ref[...]
    ranks = []
    for k in range(TOP_K):
        onehot = jnp.where(lane == te[:, k:k + 1], 1.0, 0.0)
        before = _dot(earlier, onehot.astype(BF16))
        ranks.append(jnp.sum(onehot * (before + base), -1, keepdims=True))
        base = base + jnp.sum(onehot, 0, keepdims=True)
    acc_ref[...] = base
    rank_ref[...] = jnp.concatenate(ranks, -1).astype(I32)
    cnt_ref[...] = base.astype(I32)


def _rank(top_e, tm):
    T = top_e.shape[0]
    tm = min(tm, T)
    return pl.pallas_call(
        _rank_kernel,
        grid=(T // tm,),
        in_specs=[pl.BlockSpec((tm, TOP_K), lambda i: (i, 0))],
        out_specs=[pl.BlockSpec((tm, TOP_K), lambda i: (i, 0)),
                   pl.BlockSpec((1, N_EXPERTS), lambda i: (0, 0))],
        out_shape=[jax.ShapeDtypeStruct((T, TOP_K), I32),
                   jax.ShapeDtypeStruct((1, N_EXPERTS), I32)],
        scratch_shapes=[pltpu.VMEM((1, N_EXPERTS), F32)],
        compiler_params=_params("arbitrary"),
        name="moe_rank",
    )(top_e)


def _dispatch_kernel(dest_hbm, hp_ref, xs_in, xs_hbm, idx_ref, isem, sem):
    del xs_in
    tm = hp_ref.shape[0]
    n = tm * TOP_K
    i = pl.program_id(0)
    idx_cp = pltpu.make_async_copy(dest_hbm.at[pl.ds(pl.multiple_of(i * n, n), n)], idx_ref, isem)
    idx_cp.start()
    idx_cp.wait()

    def issue(t, carry):
        for k in range(TOP_K):
            d = idx_ref[t * TOP_K + k]
            pltpu.make_async_copy(hp_ref.at[pl.ds(t, 1), :], xs_hbm.at[pl.ds(d, 1), :], sem).start()
        return carry

    lax.fori_loop(0, tm, issue, 0)
    for _ in range(TOP_K):
        pltpu.make_async_copy(hp_ref, xs_hbm.at[pl.ds(0, tm), :], sem).wait()


def _dispatch(dest_flat, hp, n_rows, tm):
    T = hp.shape[0]
    tm = min(tm, T)
    xs0 = jnp.zeros((n_rows, HALF), U32)
    return pl.pallas_call(
        _dispatch_kernel,
        grid=(T // tm,),
        in_specs=[pl.BlockSpec(memory_space=pl.ANY),
                  pl.BlockSpec((tm, HALF), lambda i: (i, 0)),
                  pl.BlockSpec(memory_space=pl.ANY)],
        out_specs=pl.BlockSpec(memory_space=pl.ANY),
        out_shape=jax.ShapeDtypeStruct((n_rows, HALF), U32),
        scratch_shapes=[pltpu.SMEM((tm * TOP_K,), I32), pltpu.SemaphoreType.DMA,
                        pltpu.SemaphoreType.DMA],
        input_output_aliases={2: 0},
        compiler_params=_params("arbitrary"),
        name="moe_dispatch",
    )(dest_flat, hp, xs0)


def _expert_kernel(be_ref, nv_ref, x_ref, wg_ref, bg_ref, wu_ref, bu_ref, wd_ref, bd_ref, y_ref):
    j = pl.program_id(0)

    @pl.when(j < nv_ref[0])
    def _():
        lo, hi = _unpack_bf16_pair(x_ref[...])
        x = jnp.concatenate([lo, hi], -1).astype(BF16)
        gate = jnp.minimum(_dot(x, wg_ref[0]) + bg_ref[0], SWIGLU_LIMIT)
        up = jnp.clip(_dot(x, wu_ref[0]) + bu_ref[0], -SWIGLU_LIMIT, SWIGLU_LIMIT)
        act = (up + 1.0) * (gate * jax.nn.sigmoid(SWIGLU_ALPHA * gate))
        y_ref[...] = _pack_bf16_pair(_dot(act.astype(BF16), wd_ref[0]) + bd_ref[0])

    @pl.when(j >= nv_ref[0])
    def _():
        y_ref[...] = jnp.zeros_like(y_ref)


def _experts(blk_e, n_valid, xs, wg, bg, wu, bu, wd, bd, tb):
    n_rows = xs.shape[0]
    wspec = pl.BlockSpec((1, D_MODEL, D_FF), lambda j, be, nv: (be[j], 0, 0))
    bspec = pl.BlockSpec((1, 1, D_FF), lambda j, be, nv: (be[j], 0, 0))
    return pl.pallas_call(
        _expert_kernel,
        grid_spec=pltpu.PrefetchScalarGridSpec(
            num_scalar_prefetch=2,
            grid=(n_rows // tb,),
            in_specs=[pl.BlockSpec((tb, HALF), lambda j, be, nv: (j, 0)),
                      wspec, bspec, wspec, bspec, wspec, bspec],
            out_specs=pl.BlockSpec((tb, HALF), lambda j, be, nv: (j, 0)),
        ),
        out_shape=jax.ShapeDtypeStruct((n_rows, HALF), U32),
        compiler_params=_params("arbitrary"),
        name="moe_experts",
    )(blk_e, n_valid, xs, wg, bg, wu, bu, wd, bd)


def _combine_kernel(dest_hbm, ys_hbm, h_ref, tw_ref, g2_ref, b2_ref, o_ref, idx_ref, buf_ref, isem, sem):
    tm = h_ref.shape[0]
    n = tm * TOP_K
    i = pl.program_id(0)
    idx_cp = pltpu.make_async_copy(dest_hbm.at[pl.ds(pl.multiple_of(i * n, n), n)], idx_ref, isem)
    idx_cp.start()
    idx_cp.wait()

    def issue(t, carry):
        for k in range(TOP_K):
            d = idx_ref[t * TOP_K + k]
            pltpu.make_async_copy(ys_hbm.at[pl.ds(d, 1), :], buf_ref.at[k, pl.ds(t, 1), :], sem).start()
        return carry

    lax.fori_loop(0, tm, issue, 0)
    for k in range(TOP_K):
        pltpu.make_async_copy(ys_hbm.at[pl.ds(0, tm), :], buf_ref.at[k], sem).wait()

    tw = tw_ref[...]
    f_lo = jnp.zeros((tm, HALF), F32)
    f_hi = jnp.zeros((tm, HALF), F32)
    for k in range(TOP_K):
        lo, hi = _unpack_bf16_pair(buf_ref[k])
        w = tw[:, k:k + 1]
        f_lo = f_lo + w * lo
        f_hi = f_hi + w * hi
    f = jnp.concatenate([f_lo, f_hi], -1)
    o_ref[...] = _layer_norm(DEEPNORM_ALPHA * h_ref[...] + f, g2_ref[...], b2_ref[...])


def _combine(dest_flat, ys, h, top_w, ln_g, ln_b, tm):
    T = h.shape[0]
    tm = min(tm, T)
    full = lambda a: pl.BlockSpec(a.shape, lambda i: (0,) * a.ndim)
    return pl.pallas_call(
        _combine_kernel,
        grid=(T // tm,),
        in_specs=[pl.BlockSpec(memory_space=pl.ANY), pl.BlockSpec(memory_space=pl.ANY),
                  pl.BlockSpec((tm, D_MODEL), lambda i: (i, 0)),
                  pl.BlockSpec((tm, TOP_K), lambda i: (i, 0)),
                  full(ln_g), full(ln_b)],
        out_specs=pl.BlockSpec((tm, D_MODEL), lambda i: (i, 0)),
        out_shape=jax.ShapeDtypeStruct((T, D_MODEL), F32),
        scratch_shapes=[pltpu.SMEM((tm * TOP_K,), I32), pltpu.VMEM((TOP_K, tm, HALF), U32),
                        pltpu.SemaphoreType.DMA, pltpu.SemaphoreType.DMA],
        compiler_params=_params("arbitrary"),
        name="moe_combine_ln2",
    )(dest_flat, ys, h, top_w, ln_g, ln_b)


def _rope_table(seq, dim):
    inv = ROPE_THETA ** (-jnp.arange(0, dim, 2, dtype=F32) / dim)
    ang = jnp.arange(seq, dtype=F32)[:, None] * inv[None, :]
    return jnp.cos(ang), jnp.sin(ang)


def _prepare_weights(w_in, ret_decay_logit, ret_gn_g, w_ret_o, mla_q_norm_g, w_uq, mla_kv_norm_g,
                     w_uk, w_uv, w_mla_o, w_out, ln1_g, ln1_b, router_w, router_b, exp_w_gate,
                     exp_b_gate, exp_w_up, exp_b_up, exp_w_down, exp_b_down, ln2_g, ln2_b):
    o = 0
    seg = {}
    for name, width in (("q", RET_QK_W), ("k", RET_QK_W), ("v", RET_V_W), ("g", RET_V_W),
                        ("cq", Q_LORA), ("ckv", KV_LORA), ("kpe", MLA_ROPE),
                        ("ga", D_MODEL), ("gb", D_MODEL)):
        seg[name] = w_in[:, o:o + width]
        o += width
    half = MLA_ROPE // 2
    swap = lambda w: jnp.concatenate([w[:, half:], w[:, :half]], -1)
    w_qk = jnp.concatenate([seg["q"], seg["k"]], -1).astype(BF16)
    w_vg = jnp.concatenate([seg["v"], seg["g"], seg["ga"], seg["gb"]], -1).astype(BF16)
    w_mla = jnp.concatenate([seg["cq"], seg["ckv"], seg["kpe"], swap(seg["kpe"])], -1).astype(BF16)
    wq = w_uq.reshape(Q_LORA, MLA_HEADS, MLA_NOPE + MLA_ROPE)
    wq_pe = wq[:, :, MLA_NOPE:]
    wq_ext = jnp.concatenate([wq[:, :, :MLA_NOPE], wq_pe,
                              jnp.concatenate([wq_pe[:, :, half:], wq_pe[:, :, :half]], -1)], -1)
    rw_hi = router_w.astype(BF16)
    return dict(
        w_qk=w_qk, w_vg=w_vg, w_mla=w_mla,
        log_gamma=jax.nn.log_sigmoid(ret_decay_logit.astype(F32)),
        gn_g=ret_gn_g.reshape(1, RET_V_W),
        w_ret_o=w_ret_o.astype(BF16),
        qn_g=mla_q_norm_g.reshape(1, Q_LORA), kvn_g=mla_kv_norm_g.reshape(1, KV_LORA),
        wq_ext=wq_ext.reshape(Q_LORA, MLA_HEADS * MLA_QK).astype(BF16),
        wk2=w_uk.reshape(KV_LORA, MLA_HEADS * MLA_NOPE).astype(BF16),
        wv2=w_uv.reshape(KV_LORA, MLA_HEADS * MLA_V).astype(BF16),
        w_mla_o=w_mla_o.astype(BF16), w_out=w_out.astype(BF16),
        ln1_g=ln1_g.reshape(1, D_MODEL), ln1_b=ln1_b.reshape(1, D_MODEL),
        rw_hi=rw_hi, rw_lo=(router_w - rw_hi.astype(F32)).astype(BF16),
        rb=router_b.reshape(1, N_EXPERTS),
        wg=exp_w_gate.astype(BF16), bg=exp_b_gate.reshape(N_EXPERTS, 1, D_FF),
        wu=exp_w_up.astype(BF16), bu=exp_b_up.reshape(N_EXPERTS, 1, D_FF),
        wd=exp_w_down.astype(BF16), bd=exp_b_down.reshape(N_EXPERTS, 1, D_MODEL),
        ln2_g=ln2_g.reshape(1, D_MODEL), ln2_b=ln2_b.reshape(1, D_MODEL),
    )


def _encoder_layer(x3, p):
    B, S, D = x3.shape
    T = B * S
    x = x3.reshape(T, D)

    cos_r, sin_r = _rope_table(S, RET_DK)
    cos_m, sin_m = _rope_table(S, MLA_ROPE)
    zeros = jnp.zeros((S, 128 - MLA_ROPE), F32)
    cc = jnp.concatenate([cos_m, cos_m, zeros], -1)
    ss = jnp.concatenate([-sin_m, sin_m, zeros], -1)

    qk = _qk_rope_proj(x, p["w_qk"], cos_r, sin_r, S, tm=2048)
    vg = _linear(x, p["w_vg"], BF16, tm=1024, tn=512, name="proj_vg")
    mla = _linear(x, p["w_mla"], F32, tm=1024, tn=MLA_SEG, name="proj_mla")

    y_ret = _retention(p["log_gamma"], qk, vg, p["gn_g"], B, S).reshape(T, RET_V_W)
    q, k, v = _mla_prep(mla, cc, ss, p["qn_g"], p["kvn_g"], p["wq_ext"], p["wk2"], p["wv2"], B, S, tm=512)
    o_mla = _attention(q, k, v, tq=512).reshape(T, MLA_HEADS * MLA_V)

    h, hp, top_e, top_w = _merge(x, y_ret, o_mla, vg, p["w_ret_o"], p["w_mla_o"], p["w_out"],
                                 p["ln1_g"], p["ln1_b"], p["rw_hi"], p["rw_lo"], p["rb"], tm=512)

    rank, counts = _rank(top_e, tm=512)
    tb = EXPERT_BLOCK
    n_rows = T * TOP_K + N_EXPERTS * tb
    counts = counts.reshape(N_EXPERTS)
    padded = (counts + tb - 1) // tb * tb
    pend = jnp.cumsum(padded)
    pstart = pend - padded
    dest = (pstart[top_e] + rank).reshape(T * TOP_K)
    blk_start = jnp.arange(n_rows // tb, dtype=I32) * tb
    blk_e = jnp.minimum(jnp.searchsorted(pend, blk_start, side="right"), N_EXPERTS - 1).astype(I32)
    n_valid = (pend[-1:] // tb).astype(I32)

    xs = _dispatch(dest, hp, n_rows, tm=256)
    ys = _experts(blk_e, n_valid, xs, p["wg"], p["bg"], p["wu"], p["bu"], p["wd"], p["bd"], tb)
    out = _combine(dest, ys, h, top_w, p["ln2_g"], p["ln2_b"], tm=256)
    return out.reshape(B, S, D)


def kernel(x_prompt, x_sample, w_in, ret_decay_logit, ret_gn_g, w_ret_o, mla_q_norm_g, w_uq, mla_kv_norm_g, w_uk, w_uv, w_mla_o, w_out, ln1_g, ln1_b, router_w, router_b, exp_w_gate, exp_b_gate, exp_w_up, exp_b_up, exp_w_down, exp_b_down, ln2_g, ln2_b):
    params = (w_in, ret_decay_logit, ret_gn_g, w_ret_o, mla_q_norm_g, w_uq, mla_kv_norm_g, w_uk, w_uv,
              w_mla_o, w_out, ln1_g, ln1_b, router_w, router_b, exp_w_gate, exp_b_gate, exp_w_up,
              exp_b_up, exp_w_down, exp_b_down, ln2_g, ln2_b)
    depth = w_in.shape[0]
    y_prompt, y_sample = x_prompt, x_sample
    for l in range(depth):
        p = _prepare_weights(*[w[l] for w in params])
        y_prompt = _encoder_layer(y_prompt, p)
        y_sample = _encoder_layer(y_sample, p)
    return (y_prompt, y_sample)
```

```python
import functools

import jax
import jax.numpy as jnp
from jax import lax
from jax.experimental import pallas as pl
from jax.experimental.pallas import tpu as pltpu

F32 = jnp.float32
BF16 = jnp.bfloat16
U32 = jnp.uint32
I32 = jnp.int32

D_MODEL = 1024
RET_HEADS = 4
RET_DK = 256
RET_DV = 512
RET_CHUNK = 128
MLA_HEADS = 8
MLA_NOPE = 128
MLA_ROPE = 64
MLA_V = 128
Q_LORA = 384
KV_LORA = 256
ROPE_THETA = 10000.0
N_EXPERTS = 32
TOP_K = 4
D_FF = 1024
SWIGLU_LIMIT = 7.0
SWIGLU_ALPHA = 1.702
LN_EPS = 1e-5
RMS_EPS = 1e-6
DEEPNORM_ALPHA = 2.0 ** 0.25
RET_QK_W = RET_HEADS * RET_DK
RET_V_W = RET_HEADS * RET_DV
MLA_SEG = Q_LORA + KV_LORA + 2 * MLA_ROPE
MLA_QK = 256
LOG2E = 1.4426950408889634

V7X_VMEM_LIMIT = 56 * 1024 * 1024
EXPERT_BLOCK = 512
HALF = D_MODEL // 2


def _params(*sem):
    return pltpu.CompilerParams(dimension_semantics=sem, vmem_limit_bytes=V7X_VMEM_LIMIT)


def _nt_dot(a, b):
    return lax.dot_general(a, b, (((1,), (1,)), ((), ())), preferred_element_type=F32)


def _tn_dot(a, b):
    return lax.dot_general(a, b, (((0,), (0,)), ((), ())), preferred_element_type=F32)


def _dot(a, b):
    return jnp.dot(a, b, preferred_element_type=F32)


def _pack_bf16_pair(x):
    bits = lax.bitcast_convert_type(x.astype(BF16).astype(F32), U32)
    return (bits[:, :HALF] >> 16) | bits[:, HALF:]


def _unpack_bf16_pair(p):
    lo = lax.bitcast_convert_type(p << 16, F32)
    hi = lax.bitcast_convert_type(p & jnp.uint32(0xFFFF0000), F32)
    return lo, hi


def _linear_kernel(x_ref, w_ref, o_ref, xb_ref):
    @pl.when(pl.program_id(1) == 0)
    def _():
        xb_ref[...] = x_ref[...].astype(BF16)

    o_ref[...] = _dot(xb_ref[...], w_ref[...]).astype(o_ref.dtype)


def _linear(x, w, out_dtype, tm, tn, name):
    M, K = x.shape
    N = w.shape[1]
    tm, tn = min(tm, M), min(tn, N)
    return pl.pallas_call(
        _linear_kernel,
        grid=(M // tm, N // tn),
        in_specs=[pl.BlockSpec((tm, K), lambda i, j: (i, 0)),
                  pl.BlockSpec((K, tn), lambda i, j: (0, j))],
        out_specs=pl.BlockSpec((tm, tn), lambda i, j: (i, j)),
        out_shape=jax.ShapeDtypeStruct((M, N), out_dtype),
        scratch_shapes=[pltpu.VMEM((tm, K), BF16)],
        compiler_params=_params("parallel", "arbitrary"),
        name=name,
    )(x, w)


def _qk_rope_kernel(x_ref, w_ref, cos_ref, sin_ref, o_ref, xb_ref, *, q_scale):
    j = pl.program_id(1)

    @pl.when(j == 0)
    def _():
        xb_ref[...] = x_ref[...].astype(BF16)

    y = _dot(xb_ref[...], w_ref[...])
    half = RET_DK // 2
    x1, x2 = y[:, :half], y[:, half:]
    c, s = cos_ref[...], sin_ref[...]
    scale = jnp.where(j < RET_HEADS, q_scale, 1.0).astype(F32)
    o_ref[:, :half] = ((x1 * c - x2 * s) * scale).astype(o_ref.dtype)
    o_ref[:, half:] = ((x1 * s + x2 * c) * scale).astype(o_ref.dtype)


def _qk_rope_proj(x, w_qk, cos, sin, seq, tm):
    M, K = x.shape
    N = w_qk.shape[1]
    tm = min(tm, seq)
    nseq = seq // tm
    return pl.pallas_call(
        functools.partial(_qk_rope_kernel, q_scale=RET_DK ** -0.5),
        grid=(M // tm, N // RET_DK),
        in_specs=[pl.BlockSpec((tm, K), lambda i, j: (i, 0)),
                  pl.BlockSpec((K, RET_DK), lambda i, j: (0, j)),
                  pl.BlockSpec((tm, RET_DK // 2), lambda i, j: (i % nseq, 0)),
                  pl.BlockSpec((tm, RET_DK // 2), lambda i, j: (i % nseq, 0))],
        out_specs=pl.BlockSpec((tm, RET_DK), lambda i, j: (i, j)),
        out_shape=jax.ShapeDtypeStruct((M, N), BF16),
        scratch_shapes=[pltpu.VMEM((tm, K), BF16)],
        compiler_params=_params("parallel", "arbitrary"),
        name="qk_rope_proj",
    )(x, w_qk, cos, sin)


def _retention_kernel(lg_ref, q_ref, k_ref, v_ref, g_ref, gn_ref, o_ref, rf_ref, rb_ref, sf_ref, sb_ref,
                      *, chunk):
    h = pl.program_id(1)
    S = q_ref.shape[1]
    C = chunk
    NC = S // C
    lgf = lg_ref[0, h]
    lgb = lg_ref[1, h]

    n_col = lax.broadcasted_iota(I32, (C, 1), 0).astype(F32)
    diff = (lax.broadcasted_iota(I32, (C, C), 0) - lax.broadcasted_iota(I32, (C, C), 1)).astype(F32)
    decay = jnp.where(diff >= 0, jnp.exp(lgf * jnp.maximum(diff, 0.0)),
                      jnp.exp(lgb * jnp.maximum(-diff, 0.0)))
    xi_f = jnp.exp(lgf * (n_col + 1.0))
    zeta_f = jnp.exp(lgf * (C - 1.0 - n_col))
    xi_b = jnp.exp(lgb * (C - n_col))
    zeta_b = jnp.exp(lgb * n_col)
    ones = jnp.ones((1, 1), F32)
    g_f = jnp.exp(ones * (lgf * C))
    g_b = jnp.exp(ones * (lgb * C))

    def rows(i):
        return pl.ds(pl.multiple_of(i * C, C), C)

    sf_ref[...] = jnp.zeros_like(sf_ref)
    sb_ref[...] = jnp.zeros_like(sb_ref)

    def scan(j, carry):
        i_f, i_b = j, NC - 1 - j
        rf_ref[i_f] = sf_ref[...].astype(BF16)
        rb_ref[i_b] = sb_ref[...].astype(BF16)
        kf = (k_ref[0, rows(i_f), :].astype(F32) * zeta_f).astype(BF16)
        kb = (k_ref[0, rows(i_b), :].astype(F32) * zeta_b).astype(BF16)
        sf_ref[...] = sf_ref[...] * g_f + _tn_dot(kf, v_ref[0, rows(i_f), :])
        sb_ref[...] = sb_ref[...] * g_b + _tn_dot(kb, v_ref[0, rows(i_b), :])
        return carry

    lax.fori_loop(0, NC, scan, 0, unroll=2 if NC % 2 == 0 else 1)

    gn = gn_ref[...]

    def emit(i, carry):
        r = rows(i)
        qb = q_ref[0, r, :]
        q = qb.astype(F32)
        v = v_ref[0, r, :]
        inner = _nt_dot(qb, k_ref[0, r, :]) * decay
        o = (_dot(inner.astype(BF16), v)
             + _dot((q * xi_f).astype(BF16), rf_ref[i])
             + _dot((q * xi_b).astype(BF16), rb_ref[i]))
        mu = jnp.mean(o, -1, keepdims=True)
        d = o - mu
        var = jnp.mean(d * d, -1, keepdims=True)
        on = d * lax.rsqrt(var + LN_EPS) * gn
        gate = g_ref[0, r, :].astype(F32)
        o_ref[0, r, :] = (gate * jax.nn.sigmoid(gate) * on).astype(o_ref.dtype)
        return carry

    lax.fori_loop(0, NC, emit, 0, unroll=4 if NC % 4 == 0 else 1)


def _retention(log_gamma, qk, vg, gn_g, B, S, chunk):
    qk3 = qk.reshape(B, S, 2 * RET_QK_W)
    vg3 = vg.reshape(B, S, vg.shape[1])
    H = RET_HEADS
    chunk = min(chunk, S)
    nc = S // chunk
    return pl.pallas_call(
        functools.partial(_retention_kernel, chunk=chunk),
        grid=(B, H),
        in_specs=[pl.BlockSpec(memory_space=pltpu.SMEM),
                  pl.BlockSpec((1, S, RET_DK), lambda b, h: (b, 0, h)),
                  pl.BlockSpec((1, S, RET_DK), lambda b, h: (b, 0, H + h)),
                  pl.BlockSpec((1, S, RET_DV), lambda b, h: (b, 0, h)),
                  pl.BlockSpec((1, S, RET_DV), lambda b, h: (b, 0, H + h)),
                  pl.BlockSpec((1, RET_DV), lambda b, h: (0, h))],
        out_specs=pl.BlockSpec((1, S, RET_DV), lambda b, h: (b, 0, h)),
        out_shape=jax.ShapeDtypeStruct((B, S, RET_V_W), BF16),
        scratch_shapes=[pltpu.VMEM((nc, RET_DK, RET_DV), BF16), pltpu.VMEM((nc, RET_DK, RET_DV), BF16),
                        pltpu.VMEM((RET_DK, RET_DV), F32), pltpu.VMEM((RET_DK, RET_DV), F32)],
        compiler_params=_params("parallel", "arbitrary"),
        name="retention",
    )(log_gamma, qk3, qk3, vg3, vg3, gn_g)


def _rms(x, g):
    return x * lax.rsqrt(jnp.mean(x * x, -1, keepdims=True) + RMS_EPS) * g


def _mla_prep_kernel(m_ref, cc_ref, ss_ref, qg_ref, kg_ref, wq_ref, wk_ref, wv_ref,
                     q_ref, k_ref, v_ref, *, q_scale):
    x = m_ref[...]
    cc, ss = cc_ref[...], ss_ref[...]

    def rope(pair):
        return pair * cc + pltpu.roll(pair, MLA_ROPE, 1) * ss

    c_q = _rms(x[:, :Q_LORA], qg_ref[...]).astype(BF16)
    c_kv = _rms(x[:, Q_LORA:Q_LORA + KV_LORA], kg_ref[...]).astype(BF16)
    k_pe = rope(x[:, Q_LORA + KV_LORA:]).astype(BF16)
    q_all = _dot(c_q, wq_ref[...])
    k_all = _dot(c_kv, wk_ref[...])
    v_all = _dot(c_kv, wv_ref[...])
    lane = lax.broadcasted_iota(I32, (x.shape[0], MLA_V), 1)
    ones_col = jnp.where(lane == 0, 1.0, 0.0).astype(BF16)
    for h in range(MLA_HEADS):
        qh = q_all[:, h * MLA_QK:(h + 1) * MLA_QK]
        q_ref[0, h, :, :MLA_NOPE] = (qh[:, :MLA_NOPE] * q_scale).astype(BF16)
        q_ref[0, h, :, MLA_NOPE:] = (rope(qh[:, MLA_NOPE:]) * q_scale).astype(BF16)
        k_ref[0, h, :, :MLA_NOPE] = k_all[:, h * MLA_NOPE:(h + 1) * MLA_NOPE].astype(BF16)
        k_ref[0, h, :, MLA_NOPE:] = k_pe
        v_ref[0, h, :, :MLA_V] = v_all[:, h * MLA_V:(h + 1) * MLA_V].astype(BF16)
        v_ref[0, h, :, MLA_V:] = ones_col


def _mla_prep(mla, cc, ss, qn_g, kvn_g, wq_ext, wk2, wv2, B, S, tm):
    tm = min(tm, S)
    ns = S // tm
    H = MLA_HEADS
    q_scale = (MLA_NOPE + MLA_ROPE) ** -0.5 * LOG2E
    full = lambda shape: pl.BlockSpec(shape, lambda b, i: (0,) * len(shape))
    return pl.pallas_call(
        functools.partial(_mla_prep_kernel, q_scale=q_scale),
        grid=(B, ns),
        in_specs=[pl.BlockSpec((tm, MLA_SEG), lambda b, i: (b * ns + i, 0)),
                  pl.BlockSpec((tm, 128), lambda b, i: (i, 0)),
                  pl.BlockSpec((tm, 128), lambda b, i: (i, 0)),
                  full((1, Q_LORA)), full((1, KV_LORA)),
                  full(wq_ext.shape), full(wk2.shape), full(wv2.shape)],
        out_specs=[pl.BlockSpec((1, H, tm, MLA_QK), lambda b, i: (b, 0, i, 0)),
                   pl.BlockSpec((1, H, tm, MLA_QK), lambda b, i: (b, 0, i, 0)),
                   pl.BlockSpec((1, H, tm, 2 * MLA_V), lambda b, i: (b, 0, i, 0))],
        out_shape=[jax.ShapeDtypeStruct((B, H, S, MLA_QK), BF16),
                   jax.ShapeDtypeStruct((B, H, S, MLA_QK), BF16),
                   jax.ShapeDtypeStruct((B, H, S, 2 * MLA_V), BF16)],
        compiler_params=_params("parallel", "parallel"),
        name="mla_prep",
    )(mla, cc, ss, qn_g, kvn_g, wq_ext, wk2, wv2)


def _attention_kernel(q_ref, k_ref, v_ref, o_ref, *, n_sub):
    tq = q_ref.shape[2]
    ts = tq // n_sub
    k = k_ref[0, 0]
    v = v_ref[0, 0]
    for u in range(n_sub):
        rows = slice(u * ts, (u + 1) * ts)
        s = _nt_dot(q_ref[0, 0, rows, :], k)
        m = jnp.max(s, -1, keepdims=True)
        p = jnp.exp2(s - m)
        o = _dot(p.astype(BF16), v)
        o_ref[0, rows, :] = (o[:, :MLA_V] / o[:, MLA_V:MLA_V + 1]).astype(o_ref.dtype)


def _attention(q, k, v, tq):
    B, H, S, _ = q.shape
    tq = min(tq, S)
    return pl.pallas_call(
        functools.partial(_attention_kernel, n_sub=max(tq // 256, 1)),
        grid=(B, H, S // tq),
        in_specs=[pl.BlockSpec((1, 1, tq, MLA_QK), lambda b, h, i: (b, h, i, 0)),
                  pl.BlockSpec((1, 1, S, MLA_QK), lambda b, h, i: (b, h, 0, 0)),
                  pl.BlockSpec((1, 1, S, 2 * MLA_V), lambda b, h, i: (b, h, 0, 0))],
        out_specs=pl.BlockSpec((1, tq, MLA_V), lambda b, h, i: (b, i, h)),
        out_shape=jax.ShapeDtypeStruct((B, S, H * MLA_V), BF16),
        compiler_params=_params("parallel", "parallel", "arbitrary"),
        name="attention",
    )(q, k, v)


def _layer_norm(x, g, b):
    mu = jnp.mean(x, -1, keepdims=True)
    d = x - mu
    var = jnp.mean(d * d, -1, keepdims=True)
    return d * lax.rsqrt(var + LN_EPS) * g + b


def _merge_kernel(x_ref, yr_ref, om_ref, gate_ref, wro_ref, wmo_ref, wout_ref, g1_ref, b1_ref,
                  rwh_ref, rwl_ref, rb_ref, h_ref, hp_ref, te_ref, tw_ref):
    y_a = _dot(yr_ref[...], wro_ref[...])
    y_b = _dot(om_ref[...], wmo_ref[...])
    gates = gate_ref[...].astype(F32)
    merged = (jax.nn.sigmoid(gates[:, :D_MODEL]) * y_a + jax.nn.sigmoid(gates[:, D_MODEL:]) * y_b)
    mix = _dot(merged.astype(BF16), wout_ref[...])
    h = _layer_norm(DEEPNORM_ALPHA * x_ref[...] + mix, g1_ref[...], b1_ref[...])
    h_ref[...] = h
    hp_ref[...] = _pack_bf16_pair(h)

    h_hi = h.astype(BF16)
    h_lo = (h - h_hi.astype(F32)).astype(BF16)
    logits = (_dot(h_hi, rwh_ref[...]) + _dot(h_lo, rwh_ref[...]) + _dot(h_hi, rwl_ref[...])
              + rb_ref[...])
    lane = lax.broadcasted_iota(I32, logits.shape, 1)
    vals, idxs = [], []
    for _ in range(TOP_K):
        m = jnp.max(logits, -1, keepdims=True)
        idx = jnp.min(jnp.where(logits == m, lane, N_EXPERTS), -1, keepdims=True)
        vals.append(m)
        idxs.append(idx)
        logits = jnp.where(lane == idx, -jnp.inf, logits)
    e = [jnp.exp(v - vals[0]) for v in vals]
    tot = e[0] + e[1] + e[2] + e[3]
    te_ref[...] = jnp.concatenate(idxs, -1)
    tw_ref[...] = jnp.concatenate([ek / tot for ek in e], -1)


def _merge(x, y_ret, o_mla, vg, w_ret_o, w_mla_o, w_out, ln_g, ln_b, rw_hi, rw_lo, rb, tm):
    T = x.shape[0]
    tm = min(tm, T)
    gate_blk = (2 * RET_V_W) // (2 * D_MODEL)
    full = lambda a: pl.BlockSpec(a.shape, lambda i: (0,) * a.ndim)
    row = lambda w: pl.BlockSpec((tm, w), lambda i: (i, 0))
    return pl.pallas_call(
        _merge_kernel,
        grid=(T // tm,),
        in_specs=[row(D_MODEL), row(RET_V_W), row(D_MODEL),
                  pl.BlockSpec((tm, 2 * D_MODEL), lambda i: (i, gate_blk)),
                  full(w_ret_o), full(w_mla_o), full(w_out), full(ln_g), full(ln_b),
                  full(rw_hi), full(rw_lo), full(rb)],
        out_specs=[row(D_MODEL), row(HALF), row(TOP_K), row(TOP_K)],
        out_shape=[jax.ShapeDtypeStruct((T, D_MODEL), F32),
                   jax.ShapeDtypeStruct((T, HALF), U32),
                   jax.ShapeDtypeStruct((T, TOP_K), I32),
                   jax.ShapeDtypeStruct((T, TOP_K), F32)],
        compiler_params=_params("parallel"),
        name="merge_ln1_router",
    )(x, y_ret, o_mla, vg, w_ret_o, w_mla_o, w_out, ln_g, ln_b, rw_hi, rw_lo, rb)


def _rank_kernel(te_ref, rank_ref, cnt_ref, acc_ref):
    @pl.when(pl.program_id(0) == 0)
    def _():
        acc_ref[...] = jnp.zeros_like(acc_ref)

    tm = te_ref.shape[0]
    te = te_ref[...]
    lane = lax.broadcasted_iota(I32, (tm, N_EXPERTS), 1)
    earlier = (lax.broadcasted_iota(I32, (tm, tm), 0) > lax.broadcasted_iota(I32, (tm, tm), 1))
    earlier = jnp.where(earlier, 1.0, 0.0).astype(BF16)
    base = acc_ref[...]
    ranks = []
    for k in range(TOP_K):
        onehot = jnp.where(lane == te[:, k:k + 1], 1.0, 0.0)
        before = _dot(earlier, onehot.astype(BF16))
        ranks.append(jnp.sum(onehot * (before + base), -1, keepdims=True))
        base = base + jnp.sum(onehot, 0, keepdims=True)
    acc_ref[...] = base
    rank_ref[...] = jnp.concatenate(ranks, -1).astype(I32)
    cnt_ref[...] = base.astype(I32)


def _rank(top_e, tm):
    T = top_e.shape[0]
    tm = min(tm, T)
    return pl.pallas_call(
        _rank_kernel,
        grid=(T // tm,),
        in_specs=[pl.BlockSpec((tm, TOP_K), lambda i: (i, 0))],
        out_specs=[pl.BlockSpec((tm, TOP_K), lambda i: (i, 0)),
                   pl.BlockSpec((1, N_EXPERTS), lambda i: (0, 0))],
        out_shape=[jax.ShapeDtypeStruct((T, TOP_K), I32),
                   jax.ShapeDtypeStruct((1, N_EXPERTS), I32)],
        scratch_shapes=[pltpu.VMEM((1, N_EXPERTS), F32)],
        compiler_params=_params("arbitrary"),
        name="moe_rank",
    )(top_e)


def _dispatch_kernel(pend_ref, padded_ref, dest_hbm, hp_ref, xs_hbm, idx_ref, zero_ref, isem, sem, *, tb):
    tm = hp_ref.shape[0]
    n = tm * TOP_K
    i = pl.program_id(0)

    @pl.when(i == 0)
    def _():
        zero_ref[...] = jnp.zeros_like(zero_ref)

        def zero_copy(e):
            start = pl.multiple_of(pend_ref[e] - tb, tb)
            return pltpu.make_async_copy(zero_ref, xs_hbm.at[pl.ds(start, tb), :], isem)

        def start(e, carry):
            @pl.when(padded_ref[e] > 0)
            def _():
                zero_copy(e).start()
            return carry

        def wait(e, carry):
            @pl.when(padded_ref[e] > 0)
            def _():
                zero_copy(e).wait()
            return carry

        lax.fori_loop(0, N_EXPERTS, start, 0)
        lax.fori_loop(0, N_EXPERTS, wait, 0)

        def tail_copy(j):
            return pltpu.make_async_copy(zero_ref, xs_hbm.at[pl.ds(pl.multiple_of(j * tb, tb), tb), :], isem)

        def tail_start(j, carry):
            tail_copy(j).start()
            return carry

        def tail_wait(j, carry):
            tail_copy(j).wait()
            return carry

        first_tail = lax.div(pend_ref[N_EXPERTS - 1], tb)
        lax.fori_loop(first_tail, xs_hbm.shape[0] // tb, tail_start, 0)
        lax.fori_loop(first_tail, xs_hbm.shape[0] // tb, tail_wait, 0)

    idx_cp = pltpu.make_async_copy(dest_hbm.at[pl.ds(pl.multiple_of(i * n, n), n)], idx_ref, isem)
    idx_cp.start()
    idx_cp.wait()

    def issue(t, carry):
        for k in range(TOP_K):
            d = idx_ref[t * TOP_K + k]
            pltpu.make_async_copy(hp_ref.at[pl.ds(t, 1), :], xs_hbm.at[pl.ds(d, 1), :], sem).start(priority=k % 2)
        return carry

    lax.fori_loop(0, tm, issue, 0, unroll=4)
    for _ in range(TOP_K):
        pltpu.make_async_copy(hp_ref, xs_hbm.at[pl.ds(0, tm), :], sem).wait()


def _dispatch(pend, padded, dest_flat, hp, n_rows, tm, tb):
    T = hp.shape[0]
    tm = min(tm, T)
    return pl.pallas_call(
        functools.partial(_dispatch_kernel, tb=tb),
        grid_spec=pltpu.PrefetchScalarGridSpec(
            num_scalar_prefetch=2,
            grid=(T // tm,),
            in_specs=[pl.BlockSpec(memory_space=pl.ANY),
                      pl.BlockSpec((tm, HALF), lambda i, pe, pa: (i, 0))],
            out_specs=pl.BlockSpec(memory_space=pl.ANY),
            scratch_shapes=[pltpu.SMEM((tm * TOP_K,), I32), pltpu.VMEM((tb, HALF), U32),
                            pltpu.SemaphoreType.DMA, pltpu.SemaphoreType.DMA],
        ),
        out_shape=jax.ShapeDtypeStruct((n_rows, HALF), U32),
        compiler_params=_params("arbitrary"),
        name="moe_dispatch",
    )(pend, padded, dest_flat, hp)


def _expert_kernel(be_ref, nv_ref, x_ref, wg_ref, bg_ref, wu_ref, bu_ref, wd_ref, bd_ref, y_ref):
    j = pl.program_id(0)

    @pl.when(j < nv_ref[0])
    def _():
        lo, hi = _unpack_bf16_pair(x_ref[...])
        x = jnp.concatenate([lo, hi], -1).astype(BF16)
        gate = jnp.minimum(_dot(x, wg_ref[0]) + bg_ref[0], SWIGLU_LIMIT)
        up = jnp.clip(_dot(x, wu_ref[0]) + bu_ref[0], -SWIGLU_LIMIT, SWIGLU_LIMIT)
        act = (up + 1.0) * (gate * jax.nn.sigmoid(SWIGLU_ALPHA * gate))
        y_ref[...] = _pack_bf16_pair(_dot(act.astype(BF16), wd_ref[0]) + bd_ref[0])

    @pl.when(j >= nv_ref[0])
    def _():
        y_ref[...] = jnp.zeros_like(y_ref)


def _experts(blk_e, n_valid, xs, wg, bg, wu, bu, wd, bd, tb):
    n_rows = xs.shape[0]
    wspec = pl.BlockSpec((1, D_MODEL, D_FF), lambda j, be, nv: (be[j], 0, 0))
    bspec = pl.BlockSpec((1, 1, D_FF), lambda j, be, nv: (be[j], 0, 0))
    return pl.pallas_call(
        _expert_kernel,
        grid_spec=pltpu.PrefetchScalarGridSpec(
            num_scalar_prefetch=2,
            grid=(n_rows // tb,),
            in_specs=[pl.BlockSpec((tb, HALF), lambda j, be, nv: (jnp.minimum(j, nv[0] - 1), 0)),
                      wspec, bspec, wspec, bspec, wspec, bspec],
            out_specs=pl.BlockSpec((tb, HALF), lambda j, be, nv: (j, 0)),
        ),
        out_shape=jax.ShapeDtypeStruct((n_rows, HALF), U32),
        compiler_params=_params("arbitrary"),
        name="moe_experts",
    )(blk_e, n_valid, xs, wg, bg, wu, bu, wd, bd)


COMBINE_GROUP = 32


def _combine_kernel(dest_hbm, ys_hbm, h_ref, tw_ref, g2_ref, b2_ref, o_ref, idx_ref, buf_ref, isem, sems):
    tm = h_ref.shape[0]
    n = tm * TOP_K
    G = min(COMBINE_GROUP, tm)
    i = pl.program_id(0)
    last = pl.num_programs(0) - 1
    slot = i % 2
    nxt = 1 - slot

    def fetch_idx(tile):
        cp = pltpu.make_async_copy(dest_hbm.at[pl.ds(pl.multiple_of(tile * n, n), n)], idx_ref, isem)
        cp.start()
        cp.wait()

    def issue(g, dst_slot):
        for tt in range(G):
            t = g * G + tt
            for k in range(TOP_K):
                d = idx_ref[t * TOP_K + k]
                pltpu.make_async_copy(ys_hbm.at[pl.ds(d, 1), :], buf_ref.at[dst_slot, k, pl.ds(t, 1), :],
                                      sems.at[dst_slot]).start(priority=k % 2)

    def wait_tile(s):
        for k in range(TOP_K):
            pltpu.make_async_copy(ys_hbm.at[pl.ds(0, tm), :], buf_ref.at[s, k], sems.at[s]).wait()

    @pl.when(i == 0)
    def _():
        fetch_idx(0)

        def first(g, carry):
            issue(g, 0)
            return carry

        lax.fori_loop(0, tm // G, first, 0)

    wait_tile(slot)
    fetch_idx(jnp.minimum(i + 1, last))
    g2, b2 = g2_ref[...], b2_ref[...]

    def group(g, carry):
        issue(g, nxt)
        r = pl.ds(pl.multiple_of(g * G, G), G)
        tw = tw_ref[r, :]
        f_lo = jnp.zeros((G, HALF), F32)
        f_hi = jnp.zeros((G, HALF), F32)
        for k in range(TOP_K):
            lo, hi = _unpack_bf16_pair(buf_ref[slot, k, r, :])
            w = tw[:, k:k + 1]
            f_lo = f_lo + w * lo
            f_hi = f_hi + w * hi
        f = jnp.concatenate([f_lo, f_hi], -1)
        o_ref[r, :] = _layer_norm(DEEPNORM_ALPHA * h_ref[r, :] + f, g2, b2)
        return carry

    lax.fori_loop(0, tm // G, group, 0)

    @pl.when(i == last)
    def _():
        wait_tile(nxt)


def _combine(dest_flat, ys, h, top_w, ln_g, ln_b, tm):
    T = h.shape[0]
    tm = min(tm, T)
    full = lambda a: pl.BlockSpec(a.shape, lambda i: (0,) * a.ndim)
    return pl.pallas_call(
        _combine_kernel,
        grid=(T // tm,),
        in_specs=[pl.BlockSpec(memory_space=pl.ANY), pl.BlockSpec(memory_space=pl.ANY),
                  pl.BlockSpec((tm, D_MODEL), lambda i: (i, 0)),
                  pl.BlockSpec((tm, TOP_K), lambda i: (i, 0)),
                  full(ln_g), full(ln_b)],
        out_specs=pl.BlockSpec((tm, D_MODEL), lambda i: (i, 0)),
        out_shape=jax.ShapeDtypeStruct((T, D_MODEL), F32),
        scratch_shapes=[pltpu.SMEM((tm * TOP_K,), I32), pltpu.VMEM((2, TOP_K, tm, HALF), U32),
                        pltpu.SemaphoreType.DMA, pltpu.SemaphoreType.DMA((2,))],
        compiler_params=_params("arbitrary"),
        name="moe_combine_ln2",
    )(dest_flat, ys, h, top_w, ln_g, ln_b)


def _rope_table(seq, dim):
    inv = ROPE_THETA ** (-jnp.arange(0, dim, 2, dtype=F32) / dim)
    ang = jnp.arange(seq, dtype=F32)[:, None] * inv[None, :]
    return jnp.cos(ang), jnp.sin(ang)


def _prepare_weights(w_in, ret_decay_logit, ret_gn_g, w_ret_o, mla_q_norm_g, w_uq, mla_kv_norm_g,
                     w_uk, w_uv, w_mla_o, w_out, ln1_g, ln1_b, router_w, router_b, exp_w_gate,
                     exp_b_gate, exp_w_up, exp_b_up, exp_w_down, exp_b_down, ln2_g, ln2_b):
    o = 0
    seg = {}
    for name, width in (("q", RET_QK_W), ("k", RET_QK_W), ("v", RET_V_W), ("g", RET_V_W),
                        ("cq", Q_LORA), ("ckv", KV_LORA), ("kpe", MLA_ROPE),
                        ("ga", D_MODEL), ("gb", D_MODEL)):
        seg[name] = w_in[:, o:o + width]
        o += width
    half = MLA_ROPE // 2
    swap = lambda w: jnp.concatenate([w[:, half:], w[:, :half]], -1)
    w_qk = jnp.concatenate([seg["q"], seg["k"]], -1).astype(BF16)
    w_vg = jnp.concatenate([seg["v"], seg["g"], seg["ga"], seg["gb"]], -1).astype(BF16)
    w_mla = jnp.concatenate([seg["cq"], seg["ckv"], seg["kpe"], swap(seg["kpe"])], -1).astype(BF16)
    wq = w_uq.reshape(Q_LORA, MLA_HEADS, MLA_NOPE + MLA_ROPE)
    wq_pe = wq[:, :, MLA_NOPE:]
    wq_ext = jnp.concatenate([wq[:, :, :MLA_NOPE], wq_pe,
                              jnp.concatenate([wq_pe[:, :, half:], wq_pe[:, :, :half]], -1)], -1)
    rw_hi = router_w.astype(BF16)
    return dict(
        w_qk=w_qk, w_vg=w_vg, w_mla=w_mla,
        log_gamma=jax.nn.log_sigmoid(ret_decay_logit.astype(F32)),
        gn_g=ret_gn_g.reshape(1, RET_V_W),
        w_ret_o=w_ret_o.astype(BF16),
        qn_g=mla_q_norm_g.reshape(1, Q_LORA), kvn_g=mla_kv_norm_g.reshape(1, KV_LORA),
        wq_ext=wq_ext.reshape(Q_LORA, MLA_HEADS * MLA_QK).astype(BF16),
        wk2=w_uk.reshape(KV_LORA, MLA_HEADS * MLA_NOPE).astype(BF16),
        wv2=w_uv.reshape(KV_LORA, MLA_HEADS * MLA_V).astype(BF16),
        w_mla_o=w_mla_o.astype(BF16), w_out=w_out.astype(BF16),
        ln1_g=ln1_g.reshape(1, D_MODEL), ln1_b=ln1_b.reshape(1, D_MODEL),
        rw_hi=rw_hi, rw_lo=(router_w - rw_hi.astype(F32)).astype(BF16),
        rb=router_b.reshape(1, N_EXPERTS),
        wg=exp_w_gate.astype(BF16), bg=exp_b_gate.reshape(N_EXPERTS, 1, D_FF),
        wu=exp_w_up.astype(BF16), bu=exp_b_up.reshape(N_EXPERTS, 1, D_FF),
        wd=exp_w_down.astype(BF16), bd=exp_b_down.reshape(N_EXPERTS, 1, D_MODEL),
        ln2_g=ln2_g.reshape(1, D_MODEL), ln2_b=ln2_b.reshape(1, D_MODEL),
    )


def _encoder_layer(x3, p):
    B, S, D = x3.shape
    T = B * S
    x = x3.reshape(T, D)

    cos_r, sin_r = _rope_table(S, RET_DK)
    cos_m, sin_m = _rope_table(S, MLA_ROPE)
    zeros = jnp.zeros((S, 128 - MLA_ROPE), F32)
    cc = jnp.concatenate([cos_m, cos_m, zeros], -1)
    ss = jnp.concatenate([-sin_m, sin_m, zeros], -1)

    qk = _qk_rope_proj(x, p["w_qk"], cos_r, sin_r, S, tm=2048)
    vg = _linear(x, p["w_vg"], BF16, tm=1024, tn=512, name="proj_vg")
    mla = _linear(x, p["w_mla"], F32, tm=1024, tn=MLA_SEG, name="proj_mla")

    y_ret = _retention(p["log_gamma"], qk, vg, p["gn_g"], B, S, chunk=256).reshape(T, RET_V_W)
    q, k, v = _mla_prep(mla, cc, ss, p["qn_g"], p["kvn_g"], p["wq_ext"], p["wk2"], p["wv2"], B, S, tm=512)
    o_mla = _attention(q, k, v, tq=1024).reshape(T, MLA_HEADS * MLA_V)

    h, hp, top_e, top_w = _merge(x, y_ret, o_mla, vg, p["w_ret_o"], p["w_mla_o"], p["w_out"],
                                 p["ln1_g"], p["ln1_b"], p["rw_hi"], p["rw_lo"], p["rb"], tm=512)

    rank, counts = _rank(top_e, tm=512)
    tb = EXPERT_BLOCK
    n_rows = T * TOP_K + N_EXPERTS * tb
    counts = counts.reshape(N_EXPERTS)
    padded = (counts + tb - 1) // tb * tb
    pend = jnp.cumsum(padded)
    pstart = pend - padded
    dest = (pstart[top_e] + rank).reshape(T * TOP_K)
    blk_start = jnp.arange(n_rows // tb, dtype=I32) * tb
    blk_e = jnp.minimum(jnp.sum((pend[None, :] <= blk_start[:, None]).astype(I32), -1), N_EXPERTS - 1)
    n_valid = (pend[-1:] // tb).astype(I32)

    xs = _dispatch(pend.astype(I32), padded.astype(I32), dest, hp, n_rows, tm=512, tb=tb)
    ys = _experts(blk_e, n_valid, xs, p["wg"], p["bg"], p["wu"], p["bu"], p["wd"], p["bd"], tb)
    out = _combine(dest, ys, h, top_w, p["ln2_g"], p["ln2_b"], tm=256)
    return out.reshape(B, S, D)


def kernel(x_prompt, x_sample, w_in, ret_decay_logit, ret_gn_g, w_ret_o, mla_q_norm_g, w_uq, mla_kv_norm_g, w_uk, w_uv, w_mla_o, w_out, ln1_g, ln1_b, router_w, router_b, exp_w_gate, exp_b_gate, exp_w_up, exp_b_up, exp_w_down, exp_b_down, ln2_g, ln2_b):
    params = (w_in, ret_decay_logit, ret_gn_g, w_ret_o, mla_q_norm_g, w_uq, mla_kv_norm_g, w_uk, w_uv,
              w_mla_o, w_out, ln1_g, ln1_b, router_w, router_b, exp_w_gate, exp_b_gate, exp_w_up,
              exp_b_up, exp_w_down, exp_b_down, ln2_g, ln2_b)
    depth = w_in.shape[0]
    y_prompt, y_sample = x_prompt, x_sample
    for l in range(depth):
        p = _prepare_weights(*[w[l] for w in params])
        y_prompt = _encoder_layer(y_prompt, p)
        y_sample = _encoder_layer(y_sample, p)
    return (y_prompt, y_sample)
```

```python
import functools

import jax
import jax.numpy as jnp
from jax import lax
from jax.experimental import pallas as pl
from jax.experimental.pallas import tpu as pltpu
from jax.experimental.pallas import tpu_sc as plsc

F32 = jnp.float32
BF16 = jnp.bfloat16
U32 = jnp.uint32
I32 = jnp.int32

D_MODEL = 1024
RET_HEADS = 4
RET_DK = 256
RET_DV = 512
RET_CHUNK = 128
MLA_HEADS = 8
MLA_NOPE = 128
MLA_ROPE = 64
MLA_V = 128
Q_LORA = 384
KV_LORA = 256
ROPE_THETA = 10000.0
N_EXPERTS = 32
TOP_K = 4
D_FF = 1024
SWIGLU_LIMIT = 7.0
SWIGLU_ALPHA = 1.702
LN_EPS = 1e-5
RMS_EPS = 1e-6
DEEPNORM_ALPHA = 2.0 ** 0.25
RET_QK_W = RET_HEADS * RET_DK
RET_V_W = RET_HEADS * RET_DV
MLA_SEG = Q_LORA + KV_LORA + 2 * MLA_ROPE
MLA_QK = 256
LOG2E = 1.4426950408889634

V7X_VMEM_LIMIT = 56 * 1024 * 1024
V7X_SC_CORES = 2
V7X_SC_SUBCORES = 16
SC_GATHER_ROWS = 128
EXPERT_BLOCK = 512
HALF = D_MODEL // 2


def _params(*sem):
    return pltpu.CompilerParams(dimension_semantics=sem, vmem_limit_bytes=V7X_VMEM_LIMIT)


def _nt_dot(a, b):
    return lax.dot_general(a, b, (((1,), (1,)), ((), ())), preferred_element_type=F32)


def _tn_dot(a, b):
    return lax.dot_general(a, b, (((0,), (0,)), ((), ())), preferred_element_type=F32)


def _dot(a, b):
    return jnp.dot(a, b, preferred_element_type=F32)


def _pack_bf16_pair(x):
    bits = lax.bitcast_convert_type(x.astype(BF16).astype(F32), U32)
    return (bits[:, :HALF] >> 16) | bits[:, HALF:]


def _unpack_bf16_pair(p):
    lo = lax.bitcast_convert_type(p << 16, F32)
    hi = lax.bitcast_convert_type(p & jnp.uint32(0xFFFF0000), F32)
    return lo, hi


def _linear_kernel(x_ref, w_ref, o_ref, xb_ref):
    @pl.when(pl.program_id(1) == 0)
    def _():
        xb_ref[...] = x_ref[...].astype(BF16)

    o_ref[...] = _dot(xb_ref[...], w_ref[...]).astype(o_ref.dtype)


def _linear(x, w, out_dtype, tm, tn, name):
    M, K = x.shape
    N = w.shape[1]
    tm, tn = min(tm, M), min(tn, N)
    return pl.pallas_call(
        _linear_kernel,
        grid=(M // tm, N // tn),
        in_specs=[pl.BlockSpec((tm, K), lambda i, j: (i, 0)),
                  pl.BlockSpec((K, tn), lambda i, j: (0, j))],
        out_specs=pl.BlockSpec((tm, tn), lambda i, j: (i, j)),
        out_shape=jax.ShapeDtypeStruct((M, N), out_dtype),
        scratch_shapes=[pltpu.VMEM((tm, K), BF16)],
        compiler_params=_params("parallel", "arbitrary"),
        name=name,
    )(x, w)


def _qk_rope_kernel(x_ref, w_ref, cos_ref, sin_ref, o_ref, xb_ref, *, q_scale):
    j = pl.program_id(1)

    @pl.when(j == 0)
    def _():
        xb_ref[...] = x_ref[...].astype(BF16)

    y = _dot(xb_ref[...], w_ref[...])
    half = RET_DK // 2
    x1, x2 = y[:, :half], y[:, half:]
    c, s = cos_ref[...], sin_ref[...]
    scale = jnp.where(j < RET_HEADS, q_scale, 1.0).astype(F32)
    o_ref[:, :half] = ((x1 * c - x2 * s) * scale).astype(o_ref.dtype)
    o_ref[:, half:] = ((x1 * s + x2 * c) * scale).astype(o_ref.dtype)


def _qk_rope_proj(x, w_qk, cos, sin, seq, tm):
    M, K = x.shape
    N = w_qk.shape[1]
    tm = min(tm, seq)
    nseq = seq // tm
    return pl.pallas_call(
        functools.partial(_qk_rope_kernel, q_scale=RET_DK ** -0.5),
        grid=(M // tm, N // RET_DK),
        in_specs=[pl.BlockSpec((tm, K), lambda i, j: (i, 0)),
                  pl.BlockSpec((K, RET_DK), lambda i, j: (0, j)),
                  pl.BlockSpec((tm, RET_DK // 2), lambda i, j: (i % nseq, 0)),
                  pl.BlockSpec((tm, RET_DK // 2), lambda i, j: (i % nseq, 0))],
        out_specs=pl.BlockSpec((tm, RET_DK), lambda i, j: (i, j)),
        out_shape=jax.ShapeDtypeStruct((M, N), BF16),
        scratch_shapes=[pltpu.VMEM((tm, K), BF16)],
        compiler_params=_params("parallel", "arbitrary"),
        name="qk_rope_proj",
    )(x, w_qk, cos, sin)


def _retention_kernel(lg_ref, q_ref, k_ref, v_ref, g_ref, gn_ref, o_ref, rf_ref, rb_ref, sf_ref, sb_ref,
                      *, chunk):
    h = pl.program_id(1)
    S = q_ref.shape[1]
    C = chunk
    NC = S // C
    lgf = lg_ref[0, h]
    lgb = lg_ref[1, h]

    n_col = lax.broadcasted_iota(I32, (C, 1), 0).astype(F32)
    diff = (lax.broadcasted_iota(I32, (C, C), 0) - lax.broadcasted_iota(I32, (C, C), 1)).astype(F32)
    decay = jnp.where(diff >= 0, jnp.exp(lgf * jnp.maximum(diff, 0.0)),
                      jnp.exp(lgb * jnp.maximum(-diff, 0.0)))
    xi_f = jnp.exp(lgf * (n_col + 1.0))
    zeta_f = jnp.exp(lgf * (C - 1.0 - n_col))
    xi_b = jnp.exp(lgb * (C - n_col))
    zeta_b = jnp.exp(lgb * n_col)
    ones = jnp.ones((1, 1), F32)
    g_f = jnp.exp(ones * (lgf * C))
    g_b = jnp.exp(ones * (lgb * C))

    def rows(i):
        return pl.ds(pl.multiple_of(i * C, C), C)

    sf_ref[...] = jnp.zeros_like(sf_ref)
    sb_ref[...] = jnp.zeros_like(sb_ref)

    def scan(j, carry):
        i_f, i_b = j, NC - 1 - j
        rf_ref[i_f] = sf_ref[...].astype(BF16)
        rb_ref[i_b] = sb_ref[...].astype(BF16)
        kf = (k_ref[0, rows(i_f), :].astype(F32) * zeta_f).astype(BF16)
        kb = (k_ref[0, rows(i_b), :].astype(F32) * zeta_b).astype(BF16)
        sf_ref[...] = sf_ref[...] * g_f + _tn_dot(kf, v_ref[0, rows(i_f), :])
        sb_ref[...] = sb_ref[...] * g_b + _tn_dot(kb, v_ref[0, rows(i_b), :])
        return carry

    lax.fori_loop(0, NC, scan, 0, unroll=2 if NC % 2 == 0 else 1)

    gn = gn_ref[...]

    def emit(i, carry):
        r = rows(i)
        qb = q_ref[0, r, :]
        q = qb.astype(F32)
        v = v_ref[0, r, :]
        inner = _nt_dot(qb, k_ref[0, r, :]) * decay
        o = (_dot(inner.astype(BF16), v)
             + _dot((q * xi_f).astype(BF16), rf_ref[i])
             + _dot((q * xi_b).astype(BF16), rb_ref[i]))
        mu = jnp.mean(o, -1, keepdims=True)
        d = o - mu
        var = jnp.mean(d * d, -1, keepdims=True)
        on = d * lax.rsqrt(var + LN_EPS) * gn
        gate = g_ref[0, r, :].astype(F32)
        o_ref[0, r, :] = (gate * jax.nn.sigmoid(gate) * on).astype(o_ref.dtype)
        return carry

    lax.fori_loop(0, NC, emit, 0, unroll=4 if NC % 4 == 0 else 1)


def _retention(log_gamma, qk, vg, gn_g, B, S, chunk):
    qk3 = qk.reshape(B, S, 2 * RET_QK_W)
    vg3 = vg.reshape(B, S, vg.shape[1])
    H = RET_HEADS
    chunk = min(chunk, S)
    nc = S // chunk
    return pl.pallas_call(
        functools.partial(_retention_kernel, chunk=chunk),
        grid=(B, H),
        in_specs=[pl.BlockSpec(memory_space=pltpu.SMEM),
                  pl.BlockSpec((1, S, RET_DK), lambda b, h: (b, 0, h)),
                  pl.BlockSpec((1, S, RET_DK), lambda b, h: (b, 0, H + h)),
                  pl.BlockSpec((1, S, RET_DV), lambda b, h: (b, 0, h)),
                  pl.BlockSpec((1, S, RET_DV), lambda b, h: (b, 0, H + h)),
                  pl.BlockSpec((1, RET_DV), lambda b, h: (0, h))],
        out_specs=pl.BlockSpec((1, S, RET_DV), lambda b, h: (b, 0, h)),
        out_shape=jax.ShapeDtypeStruct((B, S, RET_V_W), BF16),
        scratch_shapes=[pltpu.VMEM((nc, RET_DK, RET_DV), BF16), pltpu.VMEM((nc, RET_DK, RET_DV), BF16),
                        pltpu.VMEM((RET_DK, RET_DV), F32), pltpu.VMEM((RET_DK, RET_DV), F32)],
        compiler_params=_params("parallel", "arbitrary"),
        name="retention",
    )(log_gamma, qk3, qk3, vg3, vg3, gn_g)


def _rms(x, g):
    return x * lax.rsqrt(jnp.mean(x * x, -1, keepdims=True) + RMS_EPS) * g


def _mla_prep_kernel(m_ref, cc_ref, ss_ref, qg_ref, kg_ref, wq_ref, wk_ref, wv_ref,
                     q_ref, k_ref, v_ref, *, q_scale):
    x = m_ref[...]
    cc, ss = cc_ref[...], ss_ref[...]

    def rope(pair):
        return pair * cc + pltpu.roll(pair, MLA_ROPE, 1) * ss

    c_q = _rms(x[:, :Q_LORA], qg_ref[...]).astype(BF16)
    c_kv = _rms(x[:, Q_LORA:Q_LORA + KV_LORA], kg_ref[...]).astype(BF16)
    k_pe = rope(x[:, Q_LORA + KV_LORA:]).astype(BF16)
    q_all = _dot(c_q, wq_ref[...])
    k_all = _dot(c_kv, wk_ref[...])
    v_all = _dot(c_kv, wv_ref[...])
    lane = lax.broadcasted_iota(I32, (x.shape[0], MLA_V), 1)
    ones_col = jnp.where(lane == 0, 1.0, 0.0).astype(BF16)
    for h in range(MLA_HEADS):
        qh = q_all[:, h * MLA_QK:(h + 1) * MLA_QK]
        q_ref[0, h, :, :MLA_NOPE] = (qh[:, :MLA_NOPE] * q_scale).astype(BF16)
        q_ref[0, h, :, MLA_NOPE:] = (rope(qh[:, MLA_NOPE:]) * q_scale).astype(BF16)
        k_ref[0, h, :, :MLA_NOPE] = k_all[:, h * MLA_NOPE:(h + 1) * MLA_NOPE].astype(BF16)
        k_ref[0, h, :, MLA_NOPE:] = k_pe
        v_ref[0, h, :, :MLA_V] = v_all[:, h * MLA_V:(h + 1) * MLA_V].astype(BF16)
        v_ref[0, h, :, MLA_V:] = ones_col


def _mla_prep(mla, cc, ss, qn_g, kvn_g, wq_ext, wk2, wv2, B, S, tm):
    tm = min(tm, S)
    ns = S // tm
    H = MLA_HEADS
    q_scale = (MLA_NOPE + MLA_ROPE) ** -0.5 * LOG2E
    full = lambda shape: pl.BlockSpec(shape, lambda b, i: (0,) * len(shape))
    return pl.pallas_call(
        functools.partial(_mla_prep_kernel, q_scale=q_scale),
        grid=(B, ns),
        in_specs=[pl.BlockSpec((tm, MLA_SEG), lambda b, i: (b * ns + i, 0)),
                  pl.BlockSpec((tm, 128), lambda b, i: (i, 0)),
                  pl.BlockSpec((tm, 128), lambda b, i: (i, 0)),
                  full((1, Q_LORA)), full((1, KV_LORA)),
                  full(wq_ext.shape), full(wk2.shape), full(wv2.shape)],
        out_specs=[pl.BlockSpec((1, H, tm, MLA_QK), lambda b, i: (b, 0, i, 0)),
                   pl.BlockSpec((1, H, tm, MLA_QK), lambda b, i: (b, 0, i, 0)),
                   pl.BlockSpec((1, H, tm, 2 * MLA_V), lambda b, i: (b, 0, i, 0))],
        out_shape=[jax.ShapeDtypeStruct((B, H, S, MLA_QK), BF16),
                   jax.ShapeDtypeStruct((B, H, S, MLA_QK), BF16),
                   jax.ShapeDtypeStruct((B, H, S, 2 * MLA_V), BF16)],
        compiler_params=_params("parallel", "parallel"),
        name="mla_prep",
    )(mla, cc, ss, qn_g, kvn_g, wq_ext, wk2, wv2)


def _attention_kernel(q_ref, k_ref, v_ref, o_ref, *, n_sub):
    tq = q_ref.shape[2]
    ts = tq // n_sub
    k = k_ref[0, 0]
    v = v_ref[0, 0]
    for u in range(n_sub):
        rows = slice(u * ts, (u + 1) * ts)
        s = _nt_dot(q_ref[0, 0, rows, :], k)
        m = jnp.max(s, -1, keepdims=True)
        p = jnp.exp2(s - m)
        o = _dot(p.astype(BF16), v)
        o_ref[0, rows, :] = (o[:, :MLA_V] / o[:, MLA_V:MLA_V + 1]).astype(o_ref.dtype)


def _attention(q, k, v, tq):
    B, H, S, _ = q.shape
    tq = min(tq, S)
    return pl.pallas_call(
        functools.partial(_attention_kernel, n_sub=max(tq // 256, 1)),
        grid=(B, H, S // tq),
        in_specs=[pl.BlockSpec((1, 1, tq, MLA_QK), lambda b, h, i: (b, h, i, 0)),
                  pl.BlockSpec((1, 1, S, MLA_QK), lambda b, h, i: (b, h, 0, 0)),
                  pl.BlockSpec((1, 1, S, 2 * MLA_V), lambda b, h, i: (b, h, 0, 0))],
        out_specs=pl.BlockSpec((1, tq, MLA_V), lambda b, h, i: (b, i, h)),
        out_shape=jax.ShapeDtypeStruct((B, S, H * MLA_V), BF16),
        compiler_params=_params("parallel", "parallel", "arbitrary"),
        name="attention",
    )(q, k, v)


def _layer_norm(x, g, b):
    mu = jnp.mean(x, -1, keepdims=True)
    d = x - mu
    var = jnp.mean(d * d, -1, keepdims=True)
    return d * lax.rsqrt(var + LN_EPS) * g + b


def _merge_kernel(x_ref, yr_ref, om_ref, gate_ref, wro_ref, wmo_ref, wout_ref, g1_ref, b1_ref,
                  rwh_ref, rwc_ref, rb_ref, h_ref, hp_ref, te_ref, tw_ref, *, n_sub):
    ts = x_ref.shape[0] // n_sub
    for u in range(n_sub):
        r = slice(u * ts, (u + 1) * ts)
        y_a = _dot(yr_ref[r, :], wro_ref[...])
        y_b = _dot(om_ref[r, :], wmo_ref[...])
        gates = gate_ref[r, :].astype(F32)
        merged = (jax.nn.sigmoid(gates[:, :D_MODEL]) * y_a + jax.nn.sigmoid(gates[:, D_MODEL:]) * y_b)
        mix = _dot(merged.astype(BF16), wout_ref[...])
        h = _layer_norm(DEEPNORM_ALPHA * x_ref[r, :] + mix, g1_ref[...], b1_ref[...])
        h_ref[r, :] = h
        hp_ref[r, :] = _pack_bf16_pair(h)

        h_hi = h.astype(BF16)
        h_lo = (h - h_hi.astype(F32)).astype(BF16)
        both = _dot(h_hi, rwc_ref[...])
        logits = both[:, :N_EXPERTS] + both[:, N_EXPERTS:] + _dot(h_lo, rwh_ref[...]) + rb_ref[...]
        lane = lax.broadcasted_iota(I32, logits.shape, 1)
        vals, idxs = [], []
        for _ in range(TOP_K):
            m = jnp.max(logits, -1, keepdims=True)
            idx = jnp.min(jnp.where(logits == m, lane, N_EXPERTS), -1, keepdims=True)
            vals.append(m)
            idxs.append(idx)
            logits = jnp.where(lane == idx, -jnp.inf, logits)
        e = [jnp.exp(v - vals[0]) for v in vals]
        tot = e[0] + e[1] + e[2] + e[3]
        te_ref[r, :] = jnp.concatenate(idxs, -1)
        tw_ref[r, :] = jnp.concatenate([ek / tot for ek in e], -1)


def _merge(x, y_ret, o_mla, vg, w_ret_o, w_mla_o, w_out, ln_g, ln_b, rw_hi, rw_cat, rb, tm):
    T = x.shape[0]
    tm = min(tm, T)
    gate_blk = (2 * RET_V_W) // (2 * D_MODEL)
    full = lambda a: pl.BlockSpec(a.shape, lambda i: (0,) * a.ndim)
    row = lambda w: pl.BlockSpec((tm, w), lambda i: (i, 0))
    return pl.pallas_call(
        functools.partial(_merge_kernel, n_sub=1),
        grid=(T // tm,),
        in_specs=[row(D_MODEL), row(RET_V_W), row(D_MODEL),
                  pl.BlockSpec((tm, 2 * D_MODEL), lambda i: (i, gate_blk)),
                  full(w_ret_o), full(w_mla_o), full(w_out), full(ln_g), full(ln_b),
                  full(rw_hi), full(rw_cat), full(rb)],
        out_specs=[row(D_MODEL), row(HALF), row(TOP_K), row(TOP_K)],
        out_shape=[jax.ShapeDtypeStruct((T, D_MODEL), F32),
                   jax.ShapeDtypeStruct((T, HALF), U32),
                   jax.ShapeDtypeStruct((T, TOP_K), I32),
                   jax.ShapeDtypeStruct((T, TOP_K), F32)],
        compiler_params=_params("parallel"),
        name="merge_ln1_router",
    )(x, y_ret, o_mla, vg, w_ret_o, w_mla_o, w_out, ln_g, ln_b, rw_hi, rw_cat, rb)


def _rank_kernel(te_ref, rank_ref, cnt_ref, acc_ref):
    @pl.when(pl.program_id(0) == 0)
    def _():
        acc_ref[...] = jnp.zeros_like(acc_ref)

    tm = te_ref.shape[0]
    te = te_ref[...]
    lane = lax.broadcasted_iota(I32, (tm, N_EXPERTS), 1)
    earlier = (lax.broadcasted_iota(I32, (tm, tm), 0) > lax.broadcasted_iota(I32, (tm, tm), 1))
    earlier = jnp.where(earlier, 1.0, 0.0).astype(BF16)
    base = acc_ref[...]
    ranks = []
    for k in range(TOP_K):
        onehot = jnp.where(lane == te[:, k:k + 1], 1.0, 0.0)
        before = _dot(earlier, onehot.astype(BF16))
        ranks.append(jnp.sum(onehot * (before + base), -1, keepdims=True))
        base = base + jnp.sum(onehot, 0, keepdims=True)
    acc_ref[...] = base
    rank_ref[...] = jnp.concatenate(ranks, -1).astype(I32)
    cnt_ref[...] = base.astype(I32)


def _rank(top_e, tm):
    T = top_e.shape[0]
    tm = min(tm, T)
    return pl.pallas_call(
        _rank_kernel,
        grid=(T // tm,),
        in_specs=[pl.BlockSpec((tm, TOP_K), lambda i: (i, 0))],
        out_specs=[pl.BlockSpec((tm, TOP_K), lambda i: (i, 0)),
                   pl.BlockSpec((1, N_EXPERTS), lambda i: (0, 0))],
        out_shape=[jax.ShapeDtypeStruct((T, TOP_K), I32),
                   jax.ShapeDtypeStruct((1, N_EXPERTS), I32)],
        scratch_shapes=[pltpu.VMEM((1, N_EXPERTS), F32)],
        compiler_params=_params("arbitrary"),
        name="moe_rank",
    )(top_e)


def _dispatch_kernel(pend_ref, padded_ref, dest_hbm, hp_ref, xs_hbm, idx_ref, zero_ref, isem, sem, *, tb):
    tm = hp_ref.shape[0]
    n = tm * TOP_K
    i = pl.program_id(0)

    @pl.when(i == 0)
    def _():
        zero_ref[...] = jnp.zeros_like(zero_ref)

        def zero_copy(e):
            start = pl.multiple_of(pend_ref[e] - tb, tb)
            return pltpu.make_async_copy(zero_ref, xs_hbm.at[pl.ds(start, tb), :], isem)

        def start(e, carry):
            @pl.when(padded_ref[e] > 0)
            def _():
                zero_copy(e).start()
            return carry

        def wait(e, carry):
            @pl.when(padded_ref[e] > 0)
            def _():
                zero_copy(e).wait()
            return carry

        lax.fori_loop(0, N_EXPERTS, start, 0)
        lax.fori_loop(0, N_EXPERTS, wait, 0)

        def tail_copy(j):
            return pltpu.make_async_copy(zero_ref, xs_hbm.at[pl.ds(pl.multiple_of(j * tb, tb), tb), :], isem)

        def tail_start(j, carry):
            tail_copy(j).start()
            return carry

        def tail_wait(j, carry):
            tail_copy(j).wait()
            return carry

        first_tail = lax.div(pend_ref[N_EXPERTS - 1], tb)
        lax.fori_loop(first_tail, xs_hbm.shape[0] // tb, tail_start, 0)
        lax.fori_loop(first_tail, xs_hbm.shape[0] // tb, tail_wait, 0)

    idx_cp = pltpu.make_async_copy(dest_hbm.at[pl.ds(pl.multiple_of(i * n, n), n)], idx_ref, isem)
    idx_cp.start()
    idx_cp.wait()

    def issue(t, carry):
        for k in range(TOP_K):
            d = idx_ref[t * TOP_K + k]
            pltpu.make_async_copy(hp_ref.at[pl.ds(t, 1), :], xs_hbm.at[pl.ds(d, 1), :], sem).start(priority=k % 2)
        return carry

    lax.fori_loop(0, tm, issue, 0, unroll=4)
    for _ in range(TOP_K):
        pltpu.make_async_copy(hp_ref, xs_hbm.at[pl.ds(0, tm), :], sem).wait()


def _dispatch(pend, padded, dest_flat, hp, n_rows, tm, tb):
    T = hp.shape[0]
    tm = min(tm, T)
    return pl.pallas_call(
        functools.partial(_dispatch_kernel, tb=tb),
        grid_spec=pltpu.PrefetchScalarGridSpec(
            num_scalar_prefetch=2,
            grid=(T // tm,),
            in_specs=[pl.BlockSpec(memory_space=pl.ANY),
                      pl.BlockSpec((tm, HALF), lambda i, pe, pa: (i, 0))],
            out_specs=pl.BlockSpec(memory_space=pl.ANY),
            scratch_shapes=[pltpu.SMEM((tm * TOP_K,), I32), pltpu.VMEM((tb, HALF), U32),
                            pltpu.SemaphoreType.DMA, pltpu.SemaphoreType.DMA],
        ),
        out_shape=jax.ShapeDtypeStruct((n_rows, HALF), U32),
        compiler_params=_params("arbitrary"),
        name="moe_dispatch",
    )(pend, padded, dest_flat, hp)


def _expert_kernel(be_ref, nv_ref, x_ref, wg_ref, bg_ref, wu_ref, bu_ref, wd_ref, bd_ref, y_ref):
    j = pl.program_id(0)

    @pl.when(j < nv_ref[0])
    def _():
        lo, hi = _unpack_bf16_pair(x_ref[...])
        x = jnp.concatenate([lo, hi], -1).astype(BF16)
        gate = jnp.minimum(_dot(x, wg_ref[0]) + bg_ref[0], SWIGLU_LIMIT)
        up = jnp.clip(_dot(x, wu_ref[0]) + bu_ref[0], -SWIGLU_LIMIT, SWIGLU_LIMIT)
        act = (up + 1.0) * (gate * jax.nn.sigmoid(SWIGLU_ALPHA * gate))
        y_ref[...] = _pack_bf16_pair(_dot(act.astype(BF16), wd_ref[0]) + bd_ref[0])

    @pl.when(j >= nv_ref[0])
    def _():
        y_ref[...] = jnp.zeros_like(y_ref)


def _experts(blk_e, n_valid, xs, wg, bg, wu, bu, wd, bd, tb):
    n_rows = xs.shape[0]
    wspec = pl.BlockSpec((1, D_MODEL, D_FF), lambda j, be, nv: (be[j], 0, 0))
    bspec = pl.BlockSpec((1, 1, D_FF), lambda j, be, nv: (be[j], 0, 0))
    return pl.pallas_call(
        _expert_kernel,
        grid_spec=pltpu.PrefetchScalarGridSpec(
            num_scalar_prefetch=2,
            grid=(n_rows // tb,),
            in_specs=[pl.BlockSpec((tb, HALF), lambda j, be, nv: (jnp.minimum(j, nv[0] - 1), 0)),
                      wspec, bspec, wspec, bspec, wspec, bspec],
            out_specs=pl.BlockSpec((tb, HALF), lambda j, be, nv: (j, 0)),
        ),
        out_shape=jax.ShapeDtypeStruct((n_rows, HALF), U32),
        compiler_params=_params("arbitrary"),
        name="moe_experts",
    )(blk_e, n_valid, xs, wg, bg, wu, bu, wd, bd)


def _sc_gather_kernel(table_hbm, idx_hbm, out_hbm, idx_v, rows_v, sem, *, rows_per_worker):
    worker = lax.axis_index("s") * V7X_SC_CORES + lax.axis_index("c")
    base = worker * rows_per_worker

    @pl.loop(0, rows_per_worker // SC_GATHER_ROWS)
    def _(j):
        off = base + j * SC_GATHER_ROWS
        pltpu.sync_copy(idx_hbm.at[pl.ds(off, SC_GATHER_ROWS)], idx_v)
        pltpu.async_copy(table_hbm.at[idx_v], rows_v, sem).wait()
        pltpu.sync_copy(rows_v, out_hbm.at[pl.ds(off, SC_GATHER_ROWS)])


def _sc_gather_rows(table, idx):
    n, width = idx.shape[0], table.shape[1]
    workers = V7X_SC_CORES * V7X_SC_SUBCORES
    assert n % (workers * SC_GATHER_ROWS) == 0
    mesh = plsc.VectorSubcoreMesh(core_axis_name="c", subcore_axis_name="s",
                                  num_cores=V7X_SC_CORES, num_subcores=V7X_SC_SUBCORES)
    return pl.kernel(
        functools.partial(_sc_gather_kernel, rows_per_worker=n // workers),
        out_type=jax.ShapeDtypeStruct((n, width), table.dtype),
        mesh=mesh,
        scratch_types=[pltpu.VMEM((SC_GATHER_ROWS,), I32), pltpu.VMEM((SC_GATHER_ROWS, width), table.dtype),
                       pltpu.SemaphoreType.DMA],
        name="moe_gather_sc",
    )(table, idx)


def _ln2_kernel(y4_ref, h_ref, tw_ref, g2_ref, b2_ref, o_ref):
    tw = tw_ref[...]
    f_lo = jnp.zeros(y4_ref.shape[1:], F32)
    f_hi = jnp.zeros(y4_ref.shape[1:], F32)
    for k in range(TOP_K):
        lo, hi = _unpack_bf16_pair(y4_ref[k])
        w = tw[:, k:k + 1]
        f_lo = f_lo + w * lo
        f_hi = f_hi + w * hi
    f = jnp.concatenate([f_lo, f_hi], -1)
    o_ref[...] = _layer_norm(DEEPNORM_ALPHA * h_ref[...] + f, g2_ref[...], b2_ref[...])


def _weighted_sum_ln2(y4, h, top_w, ln_g, ln_b, tm):
    T = h.shape[0]
    tm = min(tm, T)
    full = lambda a: pl.BlockSpec(a.shape, lambda i: (0,) * a.ndim)
    return pl.pallas_call(
        _ln2_kernel,
        grid=(T // tm,),
        in_specs=[pl.BlockSpec((TOP_K, tm, HALF), lambda i: (0, i, 0)),
                  pl.BlockSpec((tm, D_MODEL), lambda i: (i, 0)),
                  pl.BlockSpec((tm, TOP_K), lambda i: (i, 0)),
                  full(ln_g), full(ln_b)],
        out_specs=pl.BlockSpec((tm, D_MODEL), lambda i: (i, 0)),
        out_shape=jax.ShapeDtypeStruct((T, D_MODEL), F32),
        compiler_params=_params("parallel"),
        name="moe_sum_ln2",
    )(y4, h, top_w, ln_g, ln_b)


def _rope_table(seq, dim):
    inv = ROPE_THETA ** (-jnp.arange(0, dim, 2, dtype=F32) / dim)
    ang = jnp.arange(seq, dtype=F32)[:, None] * inv[None, :]
    return jnp.cos(ang), jnp.sin(ang)


def _prepare_weights(w_in, ret_decay_logit, ret_gn_g, w_ret_o, mla_q_norm_g, w_uq, mla_kv_norm_g,
                     w_uk, w_uv, w_mla_o, w_out, ln1_g, ln1_b, router_w, router_b, exp_w_gate,
                     exp_b_gate, exp_w_up, exp_b_up, exp_w_down, exp_b_down, ln2_g, ln2_b):
    o = 0
    seg = {}
    for name, width in (("q", RET_QK_W), ("k", RET_QK_W), ("v", RET_V_W), ("g", RET_V_W),
                        ("cq", Q_LORA), ("ckv", KV_LORA), ("kpe", MLA_ROPE),
                        ("ga", D_MODEL), ("gb", D_MODEL)):
        seg[name] = w_in[:, o:o + width]
        o += width
    half = MLA_ROPE // 2
    swap = lambda w: jnp.concatenate([w[:, half:], w[:, :half]], -1)
    w_qk = jnp.concatenate([seg["q"], seg["k"]], -1).astype(BF16)
    w_vg = jnp.concatenate([seg["v"], seg["g"], seg["ga"], seg["gb"]], -1).astype(BF16)
    w_mla = jnp.concatenate([seg["cq"], seg["ckv"], seg["kpe"], swap(seg["kpe"])], -1).astype(BF16)
    wq = w_uq.reshape(Q_LORA, MLA_HEADS, MLA_NOPE + MLA_ROPE)
    wq_pe = wq[:, :, MLA_NOPE:]
    wq_ext = jnp.concatenate([wq[:, :, :MLA_NOPE], wq_pe,
                              jnp.concatenate([wq_pe[:, :, half:], wq_pe[:, :, :half]], -1)], -1)
    rw_hi = router_w.astype(BF16)
    return dict(
        w_qk=w_qk, w_vg=w_vg, w_mla=w_mla,
        log_gamma=jax.nn.log_sigmoid(ret_decay_logit.astype(F32)),
        gn_g=ret_gn_g.reshape(1, RET_V_W),
        w_ret_o=w_ret_o.astype(BF16),
        qn_g=mla_q_norm_g.reshape(1, Q_LORA), kvn_g=mla_kv_norm_g.reshape(1, KV_LORA),
        wq_ext=wq_ext.reshape(Q_LORA, MLA_HEADS * MLA_QK).astype(BF16),
        wk2=w_uk.reshape(KV_LORA, MLA_HEADS * MLA_NOPE).astype(BF16),
        wv2=w_uv.reshape(KV_LORA, MLA_HEADS * MLA_V).astype(BF16),
        w_mla_o=w_mla_o.astype(BF16), w_out=w_out.astype(BF16),
        ln1_g=ln1_g.reshape(1, D_MODEL), ln1_b=ln1_b.reshape(1, D_MODEL),
        rw_hi=rw_hi,
        rw_cat=jnp.concatenate([rw_hi, (router_w - rw_hi.astype(F32)).astype(BF16)], -1),
        rb=router_b.reshape(1, N_EXPERTS),
        wg=exp_w_gate.astype(BF16), bg=exp_b_gate.reshape(N_EXPERTS, 1, D_FF),
        wu=exp_w_up.astype(BF16), bu=exp_b_up.reshape(N_EXPERTS, 1, D_FF),
        wd=exp_w_down.astype(BF16), bd=exp_b_down.reshape(N_EXPERTS, 1, D_MODEL),
        ln2_g=ln2_g.reshape(1, D_MODEL), ln2_b=ln2_b.reshape(1, D_MODEL),
    )


def _encoder_layer(x3, p):
    B, S, D = x3.shape
    T = B * S
    x = x3.reshape(T, D)

    cos_r, sin_r = _rope_table(S, RET_DK)
    cos_m, sin_m = _rope_table(S, MLA_ROPE)
    zeros = jnp.zeros((S, 128 - MLA_ROPE), F32)
    cc = jnp.concatenate([cos_m, cos_m, zeros], -1)
    ss = jnp.concatenate([-sin_m, sin_m, zeros], -1)

    qk = _qk_rope_proj(x, p["w_qk"], cos_r, sin_r, S, tm=2048)
    vg = _linear(x, p["w_vg"], BF16, tm=1024, tn=1024, name="proj_vg")
    mla = _linear(x, p["w_mla"], F32, tm=1024, tn=MLA_SEG, name="proj_mla")

    y_ret = _retention(p["log_gamma"], qk, vg, p["gn_g"], B, S, chunk=256).reshape(T, RET_V_W)
    q, k, v = _mla_prep(mla, cc, ss, p["qn_g"], p["kvn_g"], p["wq_ext"], p["wk2"], p["wv2"], B, S, tm=512)
    o_mla = _attention(q, k, v, tq=1024).reshape(T, MLA_HEADS * MLA_V)

    h, hp, top_e, top_w = _merge(x, y_ret, o_mla, vg, p["w_ret_o"], p["w_mla_o"], p["w_out"],
                                 p["ln1_g"], p["ln1_b"], p["rw_hi"], p["rw_cat"], p["rb"], tm=512)

    rank, counts = _rank(top_e, tm=512)
    tb = EXPERT_BLOCK
    n_rows = T * TOP_K + N_EXPERTS * tb
    counts = counts.reshape(N_EXPERTS)
    padded = (counts + tb - 1) // tb * tb
    pend = jnp.cumsum(padded)
    pstart = pend - padded
    dest = (pstart[top_e] + rank).reshape(T * TOP_K)
    blk_start = jnp.arange(n_rows // tb, dtype=I32) * tb
    blk_e = jnp.minimum(jnp.sum((pend[None, :] <= blk_start[:, None]).astype(I32), -1), N_EXPERTS - 1)
    n_valid = (pend[-1:] // tb).astype(I32)

    xs = _dispatch(pend.astype(I32), padded.astype(I32), dest, hp, n_rows, tm=512, tb=tb)
    ys = _experts(blk_e, n_valid, xs, p["wg"], p["bg"], p["wu"], p["bu"], p["wd"], p["bd"], tb)
    gather_idx = dest.reshape(T, TOP_K).T.reshape(T * TOP_K)
    y4 = _sc_gather_rows(lax.bitcast_convert_type(ys, I32), gather_idx)
    y4 = lax.bitcast_convert_type(y4, U32).reshape(TOP_K, T, HALF)
    out = _weighted_sum_ln2(y4, h, top_w, p["ln2_g"], p["ln2_b"], tm=512)
    return out.reshape(B, S, D)


def kernel(x_prompt, x_sample, w_in, ret_decay_logit, ret_gn_g, w_ret_o, mla_q_norm_g, w_uq, mla_kv_norm_g, w_uk, w_uv, w_mla_o, w_out, ln1_g, ln1_b, router_w, router_b, exp_w_gate, exp_b_gate, exp_w_up, exp_b_up, exp_w_down, exp_b_down, ln2_g, ln2_b):
    params = (w_in, ret_decay_logit, ret_gn_g, w_ret_o, mla_q_norm_g, w_uq, mla_kv_norm_g, w_uk, w_uv,
              w_mla_o, w_out, ln1_g, ln1_b, router_w, router_b, exp_w_gate, exp_b_gate, exp_w_up,
              exp_b_up, exp_w_down, exp_b_down, ln2_g, ln2_b)
    depth = w_in.shape[0]
    y_prompt, y_sample = x_prompt, x_sample
    for l in range(depth):
        p = _prepare_weights(*[w[l] for w in params])
        y_prompt = _encoder_layer(y_prompt, p)
        y_sample = _encoder_layer(y_sample, p)
    return (y_prompt, y_sample)
```

```python
import functools

import jax
import jax.numpy as jnp
from jax import lax
from jax.experimental import pallas as pl
from jax.experimental.pallas import tpu as pltpu
from jax.experimental.pallas import tpu_sc as plsc

F32 = jnp.float32
BF16 = jnp.bfloat16
U32 = jnp.uint32
I32 = jnp.int32

D_MODEL = 1024
RET_HEADS = 4
RET_DK = 256
RET_DV = 512
RET_CHUNK = 128
MLA_HEADS = 8
MLA_NOPE = 128
MLA_ROPE = 64
MLA_V = 128
Q_LORA = 384
KV_LORA = 256
ROPE_THETA = 10000.0
N_EXPERTS = 32
TOP_K = 4
D_FF = 1024
SWIGLU_LIMIT = 7.0
SWIGLU_ALPHA = 1.702
LN_EPS = 1e-5
RMS_EPS = 1e-6
DEEPNORM_ALPHA = 2.0 ** 0.25
RET_QK_W = RET_HEADS * RET_DK
RET_V_W = RET_HEADS * RET_DV
MLA_SEG = Q_LORA + KV_LORA + 2 * MLA_ROPE
MLA_QK = 256
LOG2E = 1.4426950408889634

V7X_VMEM_LIMIT = 56 * 1024 * 1024
V7X_SC_CORES = 2
V7X_SC_SUBCORES = 16
SC_GATHER_ROWS = 128
EXPERT_BLOCK = 512
HALF = D_MODEL // 2


def _params(*sem):
    return pltpu.CompilerParams(dimension_semantics=sem, vmem_limit_bytes=V7X_VMEM_LIMIT)


def _nt_dot(a, b):
    return lax.dot_general(a, b, (((1,), (1,)), ((), ())), preferred_element_type=F32)


def _tn_dot(a, b):
    return lax.dot_general(a, b, (((0,), (0,)), ((), ())), preferred_element_type=F32)


def _dot(a, b):
    return jnp.dot(a, b, preferred_element_type=F32)


def _pack_bf16_pair(x):
    bits = lax.bitcast_convert_type(x.astype(BF16).astype(F32), U32)
    return lax.bitcast_convert_type((bits[:, :HALF] >> 16) | bits[:, HALF:], I32)


def _unpack_bf16_pair(p):
    p = lax.bitcast_convert_type(p, U32)
    lo = lax.bitcast_convert_type(p << 16, F32)
    hi = lax.bitcast_convert_type(p & jnp.uint32(0xFFFF0000), F32)
    return lo, hi


def _linear_kernel(x_ref, w_ref, o_ref, xb_ref):
    @pl.when(pl.program_id(1) == 0)
    def _():
        xb_ref[...] = x_ref[...].astype(BF16)

    o_ref[...] = _dot(xb_ref[...], w_ref[...]).astype(o_ref.dtype)


def _linear(x, w, out_dtype, tm, tn, name):
    M, K = x.shape
    N = w.shape[1]
    tm, tn = min(tm, M), min(tn, N)
    return pl.pallas_call(
        _linear_kernel,
        grid=(M // tm, N // tn),
        in_specs=[pl.BlockSpec((tm, K), lambda i, j: (i, 0)),
                  pl.BlockSpec((K, tn), lambda i, j: (0, j))],
        out_specs=pl.BlockSpec((tm, tn), lambda i, j: (i, j)),
        out_shape=jax.ShapeDtypeStruct((M, N), out_dtype),
        scratch_shapes=[pltpu.VMEM((tm, K), BF16)],
        compiler_params=_params("parallel", "arbitrary"),
        name=name,
    )(x, w)


def _qk_rope_kernel(x_ref, w_ref, cos_ref, sin_ref, o_ref, xb_ref, *, q_scale):
    j = pl.program_id(1)

    @pl.when(j == 0)
    def _():
        xb_ref[...] = x_ref[...].astype(BF16)

    y = _dot(xb_ref[...], w_ref[...])
    half = RET_DK // 2
    x1, x2 = y[:, :half], y[:, half:]
    c, s = cos_ref[...], sin_ref[...]
    scale = jnp.where(j < RET_HEADS, q_scale, 1.0).astype(F32)
    o_ref[:, :half] = ((x1 * c - x2 * s) * scale).astype(o_ref.dtype)
    o_ref[:, half:] = ((x1 * s + x2 * c) * scale).astype(o_ref.dtype)


def _qk_rope_proj(x, w_qk, cos, sin, seq, tm):
    M, K = x.shape
    N = w_qk.shape[1]
    tm = min(tm, seq)
    nseq = seq // tm
    return pl.pallas_call(
        functools.partial(_qk_rope_kernel, q_scale=RET_DK ** -0.5),
        grid=(M // tm, N // RET_DK),
        in_specs=[pl.BlockSpec((tm, K), lambda i, j: (i, 0)),
                  pl.BlockSpec((K, RET_DK), lambda i, j: (0, j)),
                  pl.BlockSpec((tm, RET_DK // 2), lambda i, j: (i % nseq, 0)),
                  pl.BlockSpec((tm, RET_DK // 2), lambda i, j: (i % nseq, 0))],
        out_specs=pl.BlockSpec((tm, RET_DK), lambda i, j: (i, j)),
        out_shape=jax.ShapeDtypeStruct((M, N), BF16),
        scratch_shapes=[pltpu.VMEM((tm, K), BF16)],
        compiler_params=_params("parallel", "arbitrary"),
        name="qk_rope_proj",
    )(x, w_qk, cos, sin)


def _retention_kernel(lg_ref, q_ref, k_ref, v_ref, g_ref, gn_ref, o_ref, rf_ref, rb_ref, sf_ref, sb_ref,
                      *, chunk):
    h = pl.program_id(1)
    S = q_ref.shape[1]
    C = chunk
    NC = S // C
    lgf = lg_ref[0, h]
    lgb = lg_ref[1, h]

    n_col = lax.broadcasted_iota(I32, (C, 1), 0).astype(F32)
    diff = (lax.broadcasted_iota(I32, (C, C), 0) - lax.broadcasted_iota(I32, (C, C), 1)).astype(F32)
    decay = jnp.where(diff >= 0, jnp.exp(lgf * jnp.maximum(diff, 0.0)),
                      jnp.exp(lgb * jnp.maximum(-diff, 0.0)))
    xi_f = jnp.exp(lgf * (n_col + 1.0))
    zeta_f = jnp.exp(lgf * (C - 1.0 - n_col))
    xi_b = jnp.exp(lgb * (C - n_col))
    zeta_b = jnp.exp(lgb * n_col)
    ones = jnp.ones((1, 1), F32)
    g_f = jnp.exp(ones * (lgf * C))
    g_b = jnp.exp(ones * (lgb * C))

    def rows(i):
        return pl.ds(pl.multiple_of(i * C, C), C)

    sf_ref[...] = jnp.zeros_like(sf_ref)
    sb_ref[...] = jnp.zeros_like(sb_ref)

    def scan(j, carry):
        i_f, i_b = j, NC - 1 - j
        rf_ref[i_f] = sf_ref[...].astype(BF16)
        rb_ref[i_b] = sb_ref[...].astype(BF16)
        kf = (k_ref[0, rows(i_f), :].astype(F32) * zeta_f).astype(BF16)
        kb = (k_ref[0, rows(i_b), :].astype(F32) * zeta_b).astype(BF16)
        sf_ref[...] = sf_ref[...] * g_f + _tn_dot(kf, v_ref[0, rows(i_f), :])
        sb_ref[...] = sb_ref[...] * g_b + _tn_dot(kb, v_ref[0, rows(i_b), :])
        return carry

    lax.fori_loop(0, NC, scan, 0, unroll=2 if NC % 2 == 0 else 1)

    gn = gn_ref[...]

    def emit(i, carry):
        r = rows(i)
        qb = q_ref[0, r, :]
        q = qb.astype(F32)
        v = v_ref[0, r, :]
        inner = _nt_dot(qb, k_ref[0, r, :]) * decay
        o = (_dot(inner.astype(BF16), v)
             + _dot((q * xi_f).astype(BF16), rf_ref[i])
             + _dot((q * xi_b).astype(BF16), rb_ref[i]))
        mu = jnp.mean(o, -1, keepdims=True)
        d = o - mu
        var = jnp.mean(d * d, -1, keepdims=True)
        on = d * lax.rsqrt(var + LN_EPS) * gn
        gate = g_ref[0, r, :].astype(F32)
        o_ref[0, r, :] = (gate * jax.nn.sigmoid(gate) * on).astype(o_ref.dtype)
        return carry

    lax.fori_loop(0, NC, emit, 0, unroll=4 if NC % 4 == 0 else 1)


def _retention(log_gamma, qk, vg, gn_g, B, S, chunk):
    qk3 = qk.reshape(B, S, 2 * RET_QK_W)
    vg3 = vg.reshape(B, S, vg.shape[1])
    H = RET_HEADS
    chunk = min(chunk, S)
    nc = S // chunk
    return pl.pallas_call(
        functools.partial(_retention_kernel, chunk=chunk),
        grid=(B, H),
        in_specs=[pl.BlockSpec(memory_space=pltpu.SMEM),
                  pl.BlockSpec((1, S, RET_DK), lambda b, h: (b, 0, h)),
                  pl.BlockSpec((1, S, RET_DK), lambda b, h: (b, 0, H + h)),
                  pl.BlockSpec((1, S, RET_DV), lambda b, h: (b, 0, h)),
                  pl.BlockSpec((1, S, RET_DV), lambda b, h: (b, 0, H + h)),
                  pl.BlockSpec((1, RET_DV), lambda b, h: (0, h))],
        out_specs=pl.BlockSpec((1, S, RET_DV), lambda b, h: (b, 0, h)),
        out_shape=jax.ShapeDtypeStruct((B, S, RET_V_W), BF16),
        scratch_shapes=[pltpu.VMEM((nc, RET_DK, RET_DV), BF16), pltpu.VMEM((nc, RET_DK, RET_DV), BF16),
                        pltpu.VMEM((RET_DK, RET_DV), F32), pltpu.VMEM((RET_DK, RET_DV), F32)],
        compiler_params=_params("parallel", "arbitrary"),
        name="retention",
    )(log_gamma, qk3, qk3, vg3, vg3, gn_g)


def _rms(x, g):
    return x * lax.rsqrt(jnp.mean(x * x, -1, keepdims=True) + RMS_EPS) * g


def _mla_prep_kernel(m_ref, cc_ref, ss_ref, qg_ref, kg_ref, wq_ref, wk_ref, wv_ref,
                     q_ref, k_ref, v_ref, *, q_scale):
    x = m_ref[...]
    cc, ss = cc_ref[...], ss_ref[...]

    def rope(pair):
        return pair * cc + pltpu.roll(pair, MLA_ROPE, 1) * ss

    c_q = _rms(x[:, :Q_LORA], qg_ref[...]).astype(BF16)
    c_kv = _rms(x[:, Q_LORA:Q_LORA + KV_LORA], kg_ref[...]).astype(BF16)
    k_pe = rope(x[:, Q_LORA + KV_LORA:]).astype(BF16)
    q_all = _dot(c_q, wq_ref[...])
    k_all = _dot(c_kv, wk_ref[...])
    v_all = _dot(c_kv, wv_ref[...])
    lane = lax.broadcasted_iota(I32, (x.shape[0], MLA_V), 1)
    ones_col = jnp.where(lane == 0, 1.0, 0.0).astype(BF16)
    for h in range(MLA_HEADS):
        qh = q_all[:, h * MLA_QK:(h + 1) * MLA_QK]
        q_ref[0, h, :, :MLA_NOPE] = (qh[:, :MLA_NOPE] * q_scale).astype(BF16)
        q_ref[0, h, :, MLA_NOPE:] = (rope(qh[:, MLA_NOPE:]) * q_scale).astype(BF16)
        k_ref[0, h, :, :MLA_NOPE] = k_all[:, h * MLA_NOPE:(h + 1) * MLA_NOPE].astype(BF16)
        k_ref[0, h, :, MLA_NOPE:] = k_pe
        v_ref[0, h, :, :MLA_V] = v_all[:, h * MLA_V:(h + 1) * MLA_V].astype(BF16)
        v_ref[0, h, :, MLA_V:] = ones_col


def _mla_prep(mla, cc, ss, qn_g, kvn_g, wq_ext, wk2, wv2, B, S, tm):
    tm = min(tm, S)
    ns = S // tm
    H = MLA_HEADS
    q_scale = (MLA_NOPE + MLA_ROPE) ** -0.5 * LOG2E
    full = lambda shape: pl.BlockSpec(shape, lambda b, i: (0,) * len(shape))
    return pl.pallas_call(
        functools.partial(_mla_prep_kernel, q_scale=q_scale),
        grid=(B, ns),
        in_specs=[pl.BlockSpec((tm, MLA_SEG), lambda b, i: (b * ns + i, 0)),
                  pl.BlockSpec((tm, 128), lambda b, i: (i, 0)),
                  pl.BlockSpec((tm, 128), lambda b, i: (i, 0)),
                  full((1, Q_LORA)), full((1, KV_LORA)),
                  full(wq_ext.shape), full(wk2.shape), full(wv2.shape)],
        out_specs=[pl.BlockSpec((1, H, tm, MLA_QK), lambda b, i: (b, 0, i, 0)),
                   pl.BlockSpec((1, H, tm, MLA_QK), lambda b, i: (b, 0, i, 0)),
                   pl.BlockSpec((1, H, tm, 2 * MLA_V), lambda b, i: (b, 0, i, 0))],
        out_shape=[jax.ShapeDtypeStruct((B, H, S, MLA_QK), BF16),
                   jax.ShapeDtypeStruct((B, H, S, MLA_QK), BF16),
                   jax.ShapeDtypeStruct((B, H, S, 2 * MLA_V), BF16)],
        compiler_params=_params("parallel", "parallel"),
        name="mla_prep",
    )(mla, cc, ss, qn_g, kvn_g, wq_ext, wk2, wv2)


def _attention_kernel(q_ref, k_ref, v_ref, o_ref, *, n_sub):
    tq = q_ref.shape[2]
    ts = tq // n_sub
    k = k_ref[0, 0]
    v = v_ref[0, 0]
    for u in range(n_sub):
        rows = slice(u * ts, (u + 1) * ts)
        s = _nt_dot(q_ref[0, 0, rows, :], k)
        m = jnp.max(s, -1, keepdims=True)
        p = jnp.exp2(s - m)
        o = _dot(p.astype(BF16), v)
        o_ref[0, rows, :] = (o[:, :MLA_V] / o[:, MLA_V:MLA_V + 1]).astype(o_ref.dtype)


def _attention(q, k, v, tq):
    B, H, S, _ = q.shape
    tq = min(tq, S)
    return pl.pallas_call(
        functools.partial(_attention_kernel, n_sub=max(tq // 256, 1)),
        grid=(B, H, S // tq),
        in_specs=[pl.BlockSpec((1, 1, tq, MLA_QK), lambda b, h, i: (b, h, i, 0)),
                  pl.BlockSpec((1, 1, S, MLA_QK), lambda b, h, i: (b, h, 0, 0)),
                  pl.BlockSpec((1, 1, S, 2 * MLA_V), lambda b, h, i: (b, h, 0, 0))],
        out_specs=pl.BlockSpec((1, tq, MLA_V), lambda b, h, i: (b, i, h)),
        out_shape=jax.ShapeDtypeStruct((B, S, H * MLA_V), BF16),
        compiler_params=_params("parallel", "parallel", "arbitrary"),
        name="attention",
    )(q, k, v)


def _layer_norm(x, g, b):
    mu = jnp.mean(x, -1, keepdims=True)
    d = x - mu
    var = jnp.mean(d * d, -1, keepdims=True)
    return d * lax.rsqrt(var + LN_EPS) * g + b


def _merge_kernel(x_ref, yr_ref, om_ref, gate_ref, wro_ref, wmo_ref, wout_ref, g1_ref, b1_ref,
                  rwh_ref, rwc_ref, rb_ref, h_ref, hp_ref, te_ref, tw_ref, *, n_sub):
    ts = x_ref.shape[0] // n_sub
    for u in range(n_sub):
        r = slice(u * ts, (u + 1) * ts)
        y_a = _dot(yr_ref[r, :], wro_ref[...])
        y_b = _dot(om_ref[r, :], wmo_ref[...])
        gates = gate_ref[r, :].astype(F32)
        merged = (jax.nn.sigmoid(gates[:, :D_MODEL]) * y_a + jax.nn.sigmoid(gates[:, D_MODEL:]) * y_b)
        mix = _dot(merged.astype(BF16), wout_ref[...])
        h = _layer_norm(DEEPNORM_ALPHA * x_ref[r, :] + mix, g1_ref[...], b1_ref[...])
        h_ref[r, :] = h
        hp_ref[r, :] = _pack_bf16_pair(h)

        h_hi = h.astype(BF16)
        h_lo = (h - h_hi.astype(F32)).astype(BF16)
        both = _dot(h_hi, rwc_ref[...])
        logits = both[:, :N_EXPERTS] + both[:, N_EXPERTS:] + _dot(h_lo, rwh_ref[...]) + rb_ref[...]
        lane = lax.broadcasted_iota(I32, logits.shape, 1)
        vals, idxs = [], []
        for _ in range(TOP_K):
            m = jnp.max(logits, -1, keepdims=True)
            idx = jnp.min(jnp.where(logits == m, lane, N_EXPERTS), -1, keepdims=True)
            vals.append(m)
            idxs.append(idx)
            logits = jnp.where(lane == idx, -jnp.inf, logits)
        e = [jnp.exp(v - vals[0]) for v in vals]
        tot = e[0] + e[1] + e[2] + e[3]
        te_ref[r, :] = jnp.concatenate(idxs, -1)
        tw_ref[r, :] = jnp.concatenate([ek / tot for ek in e], -1)


def _merge(x, y_ret, o_mla, vg, w_ret_o, w_mla_o, w_out, ln_g, ln_b, rw_hi, rw_cat, rb, tm):
    T = x.shape[0]
    tm = min(tm, T)
    gate_blk = (2 * RET_V_W) // (2 * D_MODEL)
    full = lambda a: pl.BlockSpec(a.shape, lambda i: (0,) * a.ndim)
    row = lambda w: pl.BlockSpec((tm, w), lambda i: (i, 0))
    return pl.pallas_call(
        functools.partial(_merge_kernel, n_sub=1),
        grid=(T // tm,),
        in_specs=[row(D_MODEL), row(RET_V_W), row(D_MODEL),
                  pl.BlockSpec((tm, 2 * D_MODEL), lambda i: (i, gate_blk)),
                  full(w_ret_o), full(w_mla_o), full(w_out), full(ln_g), full(ln_b),
                  full(rw_hi), full(rw_cat), full(rb)],
        out_specs=[row(D_MODEL), row(HALF), row(TOP_K), row(TOP_K)],
        out_shape=[jax.ShapeDtypeStruct((T, D_MODEL), F32),
                   jax.ShapeDtypeStruct((T, HALF), I32),
                   jax.ShapeDtypeStruct((T, TOP_K), I32),
                   jax.ShapeDtypeStruct((T, TOP_K), F32)],
        compiler_params=_params("parallel"),
        name="merge_ln1_router",
    )(x, y_ret, o_mla, vg, w_ret_o, w_mla_o, w_out, ln_g, ln_b, rw_hi, rw_cat, rb)


def _rank_kernel(te_ref, rank_ref, cnt_ref, acc_ref):
    @pl.when(pl.program_id(0) == 0)
    def _():
        acc_ref[...] = jnp.zeros_like(acc_ref)

    tm = te_ref.shape[0]
    te = te_ref[...]
    lane = lax.broadcasted_iota(I32, (tm, N_EXPERTS), 1)
    earlier = (lax.broadcasted_iota(I32, (tm, tm), 0) > lax.broadcasted_iota(I32, (tm, tm), 1))
    earlier = jnp.where(earlier, 1.0, 0.0).astype(BF16)
    base = acc_ref[...]
    ranks = []
    for k in range(TOP_K):
        onehot = jnp.where(lane == te[:, k:k + 1], 1.0, 0.0)
        before = _dot(earlier, onehot.astype(BF16))
        ranks.append(jnp.sum(onehot * (before + base), -1, keepdims=True))
        base = base + jnp.sum(onehot, 0, keepdims=True)
    acc_ref[...] = base
    rank_ref[...] = jnp.concatenate(ranks, -1).astype(I32)
    cnt_ref[...] = base.astype(I32)


def _rank(top_e, tm):
    T = top_e.shape[0]
    tm = min(tm, T)
    return pl.pallas_call(
        _rank_kernel,
        grid=(T // tm,),
        in_specs=[pl.BlockSpec((tm, TOP_K), lambda i: (i, 0))],
        out_specs=[pl.BlockSpec((tm, TOP_K), lambda i: (i, 0)),
                   pl.BlockSpec((1, N_EXPERTS), lambda i: (0, 0))],
        out_shape=[jax.ShapeDtypeStruct((T, TOP_K), I32),
                   jax.ShapeDtypeStruct((1, N_EXPERTS), I32)],
        scratch_shapes=[pltpu.VMEM((1, N_EXPERTS), F32)],
        compiler_params=_params("arbitrary"),
        name="moe_rank",
    )(top_e)


def _sc_mesh():
    return plsc.VectorSubcoreMesh(core_axis_name="c", subcore_axis_name="s",
                                  num_cores=V7X_SC_CORES, num_subcores=V7X_SC_SUBCORES)


def _sc_worker():
    return lax.axis_index("s") * V7X_SC_CORES + lax.axis_index("c")


def _sc_dispatch_kernel(hp_hbm, dest_hbm, pad_hbm, zeros_hbm, xs_hbm, idx_v, rows_v, sem,
                        *, chunks_per_worker, pad_chunks_per_worker):
    R = SC_GATHER_ROWS
    worker = _sc_worker()

    @pl.loop(0, chunks_per_worker)
    def _(j):
        chunk = worker * chunks_per_worker + j
        pltpu.sync_copy(hp_hbm.at[pl.ds(chunk * R, R)], rows_v)
        pltpu.sync_copy(dest_hbm.at[chunk], idx_v)
        copies = [pltpu.async_copy(rows_v, xs_hbm.at[idx_v.at[k]], sem) for k in range(TOP_K)]
        for cp in copies:
            cp.wait()

    pltpu.sync_copy(zeros_hbm, rows_v)

    @pl.loop(0, pad_chunks_per_worker)
    def _(j):
        chunk = worker * pad_chunks_per_worker + j
        pltpu.sync_copy(pad_hbm.at[chunk], idx_v.at[0])
        pltpu.async_copy(rows_v, xs_hbm.at[idx_v.at[0]], sem).wait()


def _sc_dispatch(hp, dest_chunks, pad_chunks, n_rows):
    R = SC_GATHER_ROWS
    workers = V7X_SC_CORES * V7X_SC_SUBCORES
    n_chunks, n_pad_chunks = dest_chunks.shape[0], pad_chunks.shape[0]
    assert n_chunks % workers == 0 and n_pad_chunks % workers == 0
    return pl.kernel(
        functools.partial(_sc_dispatch_kernel, chunks_per_worker=n_chunks // workers,
                          pad_chunks_per_worker=n_pad_chunks // workers),
        out_type=jax.ShapeDtypeStruct((n_rows, HALF), I32),
        mesh=_sc_mesh(),
        scratch_types=[pltpu.VMEM((TOP_K, R), I32), pltpu.VMEM((R, HALF), I32), pltpu.SemaphoreType.DMA],
        name="moe_dispatch_sc",
    )(hp, dest_chunks, pad_chunks, jnp.zeros((R, HALF), I32))


def _expert_kernel(be_ref, nv_ref, x_ref, wg_ref, bg_ref, wu_ref, bu_ref, wd_ref, bd_ref, y_ref):
    j = pl.program_id(0)

    @pl.when(j < nv_ref[0])
    def _():
        lo, hi = _unpack_bf16_pair(x_ref[...])
        x = jnp.concatenate([lo, hi], -1).astype(BF16)
        gate = jnp.minimum(_dot(x, wg_ref[0]) + bg_ref[0], SWIGLU_LIMIT)
        up = jnp.clip(_dot(x, wu_ref[0]) + bu_ref[0], -SWIGLU_LIMIT, SWIGLU_LIMIT)
        act = (up + 1.0) * (gate * jax.nn.sigmoid(SWIGLU_ALPHA * gate))
        y_ref[...] = _pack_bf16_pair(_dot(act.astype(BF16), wd_ref[0]) + bd_ref[0])

    @pl.when(j >= nv_ref[0])
    def _():
        y_ref[...] = jnp.zeros_like(y_ref)


def _experts(blk_e, n_valid, xs, wg, bg, wu, bu, wd, bd, tb):
    n_rows = xs.shape[0]
    wspec = pl.BlockSpec((1, D_MODEL, D_FF), lambda j, be, nv: (be[j], 0, 0))
    bspec = pl.BlockSpec((1, 1, D_FF), lambda j, be, nv: (be[j], 0, 0))
    return pl.pallas_call(
        _expert_kernel,
        grid_spec=pltpu.PrefetchScalarGridSpec(
            num_scalar_prefetch=2,
            grid=(n_rows // tb,),
            in_specs=[pl.BlockSpec((tb, HALF), lambda j, be, nv: (jnp.minimum(j, nv[0] - 1), 0)),
                      wspec, bspec, wspec, bspec, wspec, bspec],
            out_specs=pl.BlockSpec((tb, HALF), lambda j, be, nv: (j, 0)),
        ),
        out_shape=jax.ShapeDtypeStruct((n_rows, HALF), I32),
        compiler_params=_params("arbitrary"),
        name="moe_experts",
    )(blk_e, n_valid, xs, wg, bg, wu, bu, wd, bd)


def _sc_gather_kernel(table_hbm, idx_hbm, out_hbm, idx_v, rows_v, sem, *, rows_per_worker):
    base = _sc_worker() * rows_per_worker

    @pl.loop(0, rows_per_worker // SC_GATHER_ROWS)
    def _(j):
        off = base + j * SC_GATHER_ROWS
        pltpu.sync_copy(idx_hbm.at[pl.ds(off, SC_GATHER_ROWS)], idx_v)
        pltpu.async_copy(table_hbm.at[idx_v], rows_v, sem).wait()
        pltpu.sync_copy(rows_v, out_hbm.at[pl.ds(off, SC_GATHER_ROWS)])


def _sc_gather_rows(table, idx):
    n, width = idx.shape[0], table.shape[1]
    workers = V7X_SC_CORES * V7X_SC_SUBCORES
    assert n % (workers * SC_GATHER_ROWS) == 0
    return pl.kernel(
        functools.partial(_sc_gather_kernel, rows_per_worker=n // workers),
        out_type=jax.ShapeDtypeStruct((n, width), table.dtype),
        mesh=_sc_mesh(),
        scratch_types=[pltpu.VMEM((SC_GATHER_ROWS,), I32), pltpu.VMEM((SC_GATHER_ROWS, width), table.dtype),
                       pltpu.SemaphoreType.DMA],
        name="moe_gather_sc",
    )(table, idx)


def _ln2_kernel(y4_ref, h_ref, tw_ref, g2_ref, b2_ref, o_ref):
    tw = tw_ref[...]
    f_lo = jnp.zeros(y4_ref.shape[1:], F32)
    f_hi = jnp.zeros(y4_ref.shape[1:], F32)
    for k in range(TOP_K):
        lo, hi = _unpack_bf16_pair(y4_ref[k])
        w = tw[:, k:k + 1]
        f_lo = f_lo + w * lo
        f_hi = f_hi + w * hi
    f = jnp.concatenate([f_lo, f_hi], -1)
    o_ref[...] = _layer_norm(DEEPNORM_ALPHA * h_ref[...] + f, g2_ref[...], b2_ref[...])


def _weighted_sum_ln2(y4, h, top_w, ln_g, ln_b, tm):
    T = h.shape[0]
    tm = min(tm, T)
    full = lambda a: pl.BlockSpec(a.shape, lambda i: (0,) * a.ndim)
    return pl.pallas_call(
        _ln2_kernel,
        grid=(T // tm,),
        in_specs=[pl.BlockSpec((TOP_K, tm, HALF), lambda i: (0, i, 0)),
                  pl.BlockSpec((tm, D_MODEL), lambda i: (i, 0)),
                  pl.BlockSpec((tm, TOP_K), lambda i: (i, 0)),
                  full(ln_g), full(ln_b)],
        out_specs=pl.BlockSpec((tm, D_MODEL), lambda i: (i, 0)),
        out_shape=jax.ShapeDtypeStruct((T, D_MODEL), F32),
        compiler_params=_params("parallel"),
        name="moe_sum_ln2",
    )(y4, h, top_w, ln_g, ln_b)


def _rope_table(seq, dim):
    inv = ROPE_THETA ** (-jnp.arange(0, dim, 2, dtype=F32) / dim)
    ang = jnp.arange(seq, dtype=F32)[:, None] * inv[None, :]
    return jnp.cos(ang), jnp.sin(ang)


def _prepare_weights(w_in, ret_decay_logit, ret_gn_g, w_ret_o, mla_q_norm_g, w_uq, mla_kv_norm_g,
                     w_uk, w_uv, w_mla_o, w_out, ln1_g, ln1_b, router_w, router_b, exp_w_gate,
                     exp_b_gate, exp_w_up, exp_b_up, exp_w_down, exp_b_down, ln2_g, ln2_b):
    o = 0
    seg = {}
    for name, width in (("q", RET_QK_W), ("k", RET_QK_W), ("v", RET_V_W), ("g", RET_V_W),
                        ("cq", Q_LORA), ("ckv", KV_LORA), ("kpe", MLA_ROPE),
                        ("ga", D_MODEL), ("gb", D_MODEL)):
        seg[name] = w_in[:, o:o + width]
        o += width
    half = MLA_ROPE // 2
    swap = lambda w: jnp.concatenate([w[:, half:], w[:, :half]], -1)
    w_qk = jnp.concatenate([seg["q"], seg["k"]], -1).astype(BF16)
    w_vg = jnp.concatenate([seg["v"], seg["g"], seg["ga"], seg["gb"]], -1).astype(BF16)
    w_mla = jnp.concatenate([seg["cq"], seg["ckv"], seg["kpe"], swap(seg["kpe"])], -1).astype(BF16)
    wq = w_uq.reshape(Q_LORA, MLA_HEADS, MLA_NOPE + MLA_ROPE)
    wq_pe = wq[:, :, MLA_NOPE:]
    wq_ext = jnp.concatenate([wq[:, :, :MLA_NOPE], wq_pe,
                              jnp.concatenate([wq_pe[:, :, half:], wq_pe[:, :, :half]], -1)], -1)
    rw_hi = router_w.astype(BF16)
    return dict(
        w_qk=w_qk, w_vg=w_vg, w_mla=w_mla,
        log_gamma=jax.nn.log_sigmoid(ret_decay_logit.astype(F32)),
        gn_g=ret_gn_g.reshape(1, RET_V_W),
        w_ret_o=w_ret_o.astype(BF16),
        qn_g=mla_q_norm_g.reshape(1, Q_LORA), kvn_g=mla_kv_norm_g.reshape(1, KV_LORA),
        wq_ext=wq_ext.reshape(Q_LORA, MLA_HEADS * MLA_QK).astype(BF16),
        wk2=w_uk.reshape(KV_LORA, MLA_HEADS * MLA_NOPE).astype(BF16),
        wv2=w_uv.reshape(KV_LORA, MLA_HEADS * MLA_V).astype(BF16),
        w_mla_o=w_mla_o.astype(BF16), w_out=w_out.astype(BF16),
        ln1_g=ln1_g.reshape(1, D_MODEL), ln1_b=ln1_b.reshape(1, D_MODEL),
        rw_hi=rw_hi,
        rw_cat=jnp.concatenate([rw_hi, (router_w - rw_hi.astype(F32)).astype(BF16)], -1),
        rb=router_b.reshape(1, N_EXPERTS),
        wg=exp_w_gate.astype(BF16), bg=exp_b_gate.reshape(N_EXPERTS, 1, D_FF),
        wu=exp_w_up.astype(BF16), bu=exp_b_up.reshape(N_EXPERTS, 1, D_FF),
        wd=exp_w_down.astype(BF16), bd=exp_b_down.reshape(N_EXPERTS, 1, D_MODEL),
        ln2_g=ln2_g.reshape(1, D_MODEL), ln2_b=ln2_b.reshape(1, D_MODEL),
    )


def _encoder_layer(x3, p):
    B, S, D = x3.shape
    T = B * S
    x = x3.reshape(T, D)

    cos_r, sin_r = _rope_table(S, RET_DK)
    cos_m, sin_m = _rope_table(S, MLA_ROPE)
    zeros = jnp.zeros((S, 128 - MLA_ROPE), F32)
    cc = jnp.concatenate([cos_m, cos_m, zeros], -1)
    ss = jnp.concatenate([-sin_m, sin_m, zeros], -1)

    qk = _qk_rope_proj(x, p["w_qk"], cos_r, sin_r, S, tm=2048)
    vg = _linear(x, p["w_vg"], BF16, tm=1024, tn=1024, name="proj_vg")
    mla = _linear(x, p["w_mla"], F32, tm=1024, tn=MLA_SEG, name="proj_mla")

    y_ret = _retention(p["log_gamma"], qk, vg, p["gn_g"], B, S, chunk=256).reshape(T, RET_V_W)
    q, k, v = _mla_prep(mla, cc, ss, p["qn_g"], p["kvn_g"], p["wq_ext"], p["wk2"], p["wv2"], B, S, tm=512)
    o_mla = _attention(q, k, v, tq=1024).reshape(T, MLA_HEADS * MLA_V)

    h, hp, top_e, top_w = _merge(x, y_ret, o_mla, vg, p["w_ret_o"], p["w_mla_o"], p["w_out"],
                                 p["ln1_g"], p["ln1_b"], p["rw_hi"], p["rw_cat"], p["rb"], tm=512)

    rank, counts = _rank(top_e, tm=512)
    tb = EXPERT_BLOCK
    n_rows = T * TOP_K + N_EXPERTS * tb
    counts = counts.reshape(N_EXPERTS)
    padded = (counts + tb - 1) // tb * tb
    pend = jnp.cumsum(padded)
    pstart = pend - padded
    dest = pstart[top_e] + rank
    blk_start = jnp.arange(n_rows // tb, dtype=I32) * tb
    blk_e = jnp.minimum(jnp.sum((pend[None, :] <= blk_start[:, None]).astype(I32), -1), N_EXPERTS - 1)
    n_valid = (pend[-1:] // tb).astype(I32)

    seg = jnp.concatenate([padded - counts, n_rows - pend[-1:]])
    seg_end = jnp.cumsum(seg)
    seg_base = jnp.concatenate([pstart + counts, pend[-1:]])
    j = jnp.arange(n_rows - T * TOP_K, dtype=I32)
    s = jnp.sum((seg_end[None, :] <= j[:, None]).astype(I32), -1)
    pad_rows = (seg_base[s] + j - (seg_end - seg)[s]).astype(I32)

    R = SC_GATHER_ROWS
    dest_chunks = dest.reshape(T // R, R, TOP_K).transpose(0, 2, 1)
    xs = _sc_dispatch(hp, dest_chunks, pad_rows.reshape(-1, R), n_rows)
    ys = _experts(blk_e, n_valid, xs, p["wg"], p["bg"], p["wu"], p["bu"], p["wd"], p["bd"], tb)
    y4 = _sc_gather_rows(ys, dest.T.reshape(T * TOP_K)).reshape(TOP_K, T, HALF)
    out = _weighted_sum_ln2(y4, h, top_w, p["ln2_g"], p["ln2_b"], tm=512)
    return out.reshape(B, S, D)


def kernel(x_prompt, x_sample, w_in, ret_decay_logit, ret_gn_g, w_ret_o, mla_q_norm_g, w_uq, mla_kv_norm_g, w_uk, w_uv, w_mla_o, w_out, ln1_g, ln1_b, router_w, router_b, exp_w_gate, exp_b_gate, exp_w_up, exp_b_up, exp_w_down, exp_b_down, ln2_g, ln2_b):
    params = (w_in, ret_decay_logit, ret_gn_g, w_ret_o, mla_q_norm_g, w_uq, mla_kv_norm_g, w_uk, w_uv,
              w_mla_o, w_out, ln1_g, ln1_b, router_w, router_b, exp_w_gate, exp_b_gate, exp_w_up,
              exp_b_up, exp_w_down, exp_b_down, ln2_g, ln2_b)
    depth = w_in.shape[0]
    y_prompt, y_sample = x_prompt, x_sample
    for l in range(depth):
        p = _prepare_weights(*[w[l] for w in params])
        y_prompt = _encoder_layer(y_prompt, p)
        y_sample = _encoder_layer(y_sample, p)
    return (y_prompt, y_sample)
```

```python
import functools

import jax
import jax.numpy as jnp
from jax import lax
from jax.experimental import pallas as pl
from jax.experimental.pallas import tpu as pltpu
from jax.experimental.pallas import tpu_sc as plsc

F32 = jnp.float32
BF16 = jnp.bfloat16
U32 = jnp.uint32
I32 = jnp.int32

D_MODEL = 1024
RET_HEADS = 4
RET_DK = 256
RET_DV = 512
RET_CHUNK = 128
MLA_HEADS = 8
MLA_NOPE = 128
MLA_ROPE = 64
MLA_V = 128
Q_LORA = 384
KV_LORA = 256
ROPE_THETA = 10000.0
N_EXPERTS = 32
TOP_K = 4
D_FF = 1024
SWIGLU_LIMIT = 7.0
SWIGLU_ALPHA = 1.702
LN_EPS = 1e-5
RMS_EPS = 1e-6
DEEPNORM_ALPHA = 2.0 ** 0.25
RET_QK_W = RET_HEADS * RET_DK
RET_V_W = RET_HEADS * RET_DV
MLA_SEG = Q_LORA + KV_LORA + 2 * MLA_ROPE
MLA_QK = 256
LOG2E = 1.4426950408889634

V7X_VMEM_LIMIT = 56 * 1024 * 1024
V7X_SC_CORES = 2
V7X_SC_SUBCORES = 16
SC_GATHER_ROWS = 128
EXPERT_BLOCK = 512
ATTN_ROW_GROUP = 256
ATTN_SCORE_BYTES = 16 * 1024 * 1024
HALF = D_MODEL // 2


def _params(*sem):
    return pltpu.CompilerParams(dimension_semantics=sem, vmem_limit_bytes=V7X_VMEM_LIMIT)


def _nt_dot(a, b):
    return lax.dot_general(a, b, (((1,), (1,)), ((), ())), preferred_element_type=F32)


def _tn_dot(a, b):
    return lax.dot_general(a, b, (((0,), (0,)), ((), ())), preferred_element_type=F32)


def _dot(a, b):
    return jnp.dot(a, b, preferred_element_type=F32)


def _pack_bf16_pair(x):
    bits = lax.bitcast_convert_type(x.astype(BF16).astype(F32), U32)
    return lax.bitcast_convert_type((bits[:, :HALF] >> 16) | bits[:, HALF:], I32)


def _unpack_bf16_pair(p):
    p = lax.bitcast_convert_type(p, U32)
    lo = lax.bitcast_convert_type(p << 16, F32)
    hi = lax.bitcast_convert_type(p & jnp.uint32(0xFFFF0000), F32)
    return lo, hi


def _linear_kernel(x_ref, w_ref, o_ref, xb_ref):
    @pl.when(pl.program_id(1) == 0)
    def _():
        xb_ref[...] = x_ref[...].astype(BF16)

    o_ref[...] = _dot(xb_ref[...], w_ref[...]).astype(o_ref.dtype)


def _linear(x, w, out_dtype, tm, tn, name):
    M, K = x.shape
    N = w.shape[1]
    tm, tn = min(tm, M), min(tn, N)
    return pl.pallas_call(
        _linear_kernel,
        grid=(M // tm, N // tn),
        in_specs=[pl.BlockSpec((tm, K), lambda i, j: (i, 0)),
                  pl.BlockSpec((K, tn), lambda i, j: (0, j))],
        out_specs=pl.BlockSpec((tm, tn), lambda i, j: (i, j)),
        out_shape=jax.ShapeDtypeStruct((M, N), out_dtype),
        scratch_shapes=[pltpu.VMEM((tm, K), BF16)],
        compiler_params=_params("parallel", "arbitrary"),
        name=name,
    )(x, w)


def _qk_rope_kernel(x_ref, w_ref, cos_ref, sin_ref, o_ref, xb_ref, *, q_scale):
    j = pl.program_id(1)

    @pl.when(j == 0)
    def _():
        xb_ref[...] = x_ref[...].astype(BF16)

    y = _dot(xb_ref[...], w_ref[...])
    half = RET_DK // 2
    heads = y.shape[1] // RET_DK
    c, s = cos_ref[...], sin_ref[...]
    scale = jnp.where(j < RET_HEADS // heads, q_scale, 1.0).astype(F32)
    for h in range(heads):
        x1 = y[:, h * RET_DK:h * RET_DK + half]
        x2 = y[:, h * RET_DK + half:(h + 1) * RET_DK]
        o_ref[:, h * RET_DK:h * RET_DK + half] = ((x1 * c - x2 * s) * scale).astype(o_ref.dtype)
        o_ref[:, h * RET_DK + half:(h + 1) * RET_DK] = ((x1 * s + x2 * c) * scale).astype(o_ref.dtype)


def _qk_rope_proj(x, w_qk, cos, sin, seq, tm):
    M, K = x.shape
    N = w_qk.shape[1]
    tm = min(tm, seq)
    nseq = seq // tm
    tn = 2 * RET_DK
    return pl.pallas_call(
        functools.partial(_qk_rope_kernel, q_scale=RET_DK ** -0.5),
        grid=(M // tm, N // tn),
        in_specs=[pl.BlockSpec((tm, K), lambda i, j: (i, 0)),
                  pl.BlockSpec((K, tn), lambda i, j: (0, j)),
                  pl.BlockSpec((tm, RET_DK // 2), lambda i, j: (i % nseq, 0)),
                  pl.BlockSpec((tm, RET_DK // 2), lambda i, j: (i % nseq, 0))],
        out_specs=pl.BlockSpec((tm, tn), lambda i, j: (i, j)),
        out_shape=jax.ShapeDtypeStruct((M, N), BF16),
        scratch_shapes=[pltpu.VMEM((tm, K), BF16)],
        compiler_params=_params("parallel", "arbitrary"),
        name="qk_rope_proj",
    )(x, w_qk, cos, sin)


def _retention_kernel(lg_ref, q_ref, k_ref, v_ref, g_ref, gn_ref, o_ref, rf_ref, rb_ref, sf_ref, sb_ref,
                      *, chunk):
    h = pl.program_id(1)
    S = q_ref.shape[1]
    C = chunk
    NC = S // C
    lgf = lg_ref[0, h]
    lgb = lg_ref[1, h]

    n_col = lax.broadcasted_iota(I32, (C, 1), 0).astype(F32)
    diff = (lax.broadcasted_iota(I32, (C, C), 0) - lax.broadcasted_iota(I32, (C, C), 1)).astype(F32)
    decay = jnp.where(diff >= 0, jnp.exp(lgf * jnp.maximum(diff, 0.0)),
                      jnp.exp(lgb * jnp.maximum(-diff, 0.0)))
    xi_f = jnp.exp(lgf * (n_col + 1.0))
    zeta_f = jnp.exp(lgf * (C - 1.0 - n_col))
    xi_b = jnp.exp(lgb * (C - n_col))
    zeta_b = jnp.exp(lgb * n_col)
    ones = jnp.ones((1, 1), F32)
    g_f = jnp.exp(ones * (lgf * C))
    g_b = jnp.exp(ones * (lgb * C))

    def rows(i):
        return pl.ds(pl.multiple_of(i * C, C), C)

    sf_ref[...] = jnp.zeros_like(sf_ref)
    sb_ref[...] = jnp.zeros_like(sb_ref)

    def scan(j, carry):
        i_f, i_b = j, NC - 1 - j
        rf_ref[i_f] = sf_ref[...].astype(BF16)
        rb_ref[i_b] = sb_ref[...].astype(BF16)
        kf = (k_ref[0, rows(i_f), :].astype(F32) * zeta_f).astype(BF16)
        kb = (k_ref[0, rows(i_b), :].astype(F32) * zeta_b).astype(BF16)
        sf_ref[...] = sf_ref[...] * g_f + _tn_dot(kf, v_ref[0, rows(i_f), :])
        sb_ref[...] = sb_ref[...] * g_b + _tn_dot(kb, v_ref[0, rows(i_b), :])
        return carry

    lax.fori_loop(0, NC, scan, 0, unroll=2 if NC % 2 == 0 else 1)

    gn = gn_ref[...]

    def emit(i, carry):
        r = rows(i)
        qb = q_ref[0, r, :]
        q = qb.astype(F32)
        v = v_ref[0, r, :]
        inner = _nt_dot(qb, k_ref[0, r, :]) * decay
        o = (_dot(inner.astype(BF16), v)
             + _dot((q * xi_f).astype(BF16), rf_ref[i])
             + _dot((q * xi_b).astype(BF16), rb_ref[i]))
        mu = jnp.mean(o, -1, keepdims=True)
        d = o - mu
        var = jnp.mean(d * d, -1, keepdims=True)
        on = d * lax.rsqrt(var + LN_EPS) * gn
        gate = g_ref[0, r, :].astype(F32)
        o_ref[0, r, :] = (gate * jax.nn.sigmoid(gate) * on).astype(o_ref.dtype)
        return carry

    lax.fori_loop(0, NC, emit, 0, unroll=4 if NC % 4 == 0 else 1)


def _retention(log_gamma, qk, vg, gn_g, B, S, chunk):
    qk3 = qk.reshape(B, S, 2 * RET_QK_W)
    vg3 = vg.reshape(B, S, vg.shape[1])
    H = RET_HEADS
    chunk = min(chunk, S)
    nc = S // chunk
    return pl.pallas_call(
        functools.partial(_retention_kernel, chunk=chunk),
        grid=(B, H),
        in_specs=[pl.BlockSpec(memory_space=pltpu.SMEM),
                  pl.BlockSpec((1, S, RET_DK), lambda b, h: (b, 0, h)),
                  pl.BlockSpec((1, S, RET_DK), lambda b, h: (b, 0, H + h)),
                  pl.BlockSpec((1, S, RET_DV), lambda b, h: (b, 0, h)),
                  pl.BlockSpec((1, S, RET_DV), lambda b, h: (b, 0, H + h)),
                  pl.BlockSpec((1, RET_DV), lambda b, h: (0, h))],
        out_specs=pl.BlockSpec((1, S, RET_DV), lambda b, h: (b, 0, h)),
        out_shape=jax.ShapeDtypeStruct((B, S, RET_V_W), BF16),
        scratch_shapes=[pltpu.VMEM((nc, RET_DK, RET_DV), BF16), pltpu.VMEM((nc, RET_DK, RET_DV), BF16),
                        pltpu.VMEM((RET_DK, RET_DV), F32), pltpu.VMEM((RET_DK, RET_DV), F32)],
        compiler_params=_params("parallel", "arbitrary"),
        name="retention",
    )(log_gamma, qk3, qk3, vg3, vg3, gn_g)


def _rms(x, g):
    return x * lax.rsqrt(jnp.mean(x * x, -1, keepdims=True) + RMS_EPS) * g


def _mla_prep_kernel(m_ref, cc_ref, ss_ref, qg_ref, kg_ref, wq_ref, wk_ref, wv_ref,
                     q_ref, k_ref, v_ref, *, q_scale):
    x = m_ref[...]
    cc, ss = cc_ref[...], ss_ref[...]

    def rope(pair):
        return pair * cc + pltpu.roll(pair, MLA_ROPE, 1) * ss

    c_q = _rms(x[:, :Q_LORA], qg_ref[...]).astype(BF16)
    c_kv = _rms(x[:, Q_LORA:Q_LORA + KV_LORA], kg_ref[...]).astype(BF16)
    k_pe = rope(x[:, Q_LORA + KV_LORA:]).astype(BF16)
    q_all = _dot(c_q, wq_ref[...])
    k_all = _dot(c_kv, wk_ref[...])
    v_all = _dot(c_kv, wv_ref[...])
    lane = lax.broadcasted_iota(I32, (x.shape[0], MLA_V), 1)
    ones_col = jnp.where(lane == 0, 1.0, 0.0).astype(BF16)
    for h in range(MLA_HEADS):
        qh = q_all[:, h * MLA_QK:(h + 1) * MLA_QK]
        q_ref[0, h, :, :MLA_NOPE] = (qh[:, :MLA_NOPE] * q_scale).astype(BF16)
        q_ref[0, h, :, MLA_NOPE:] = (rope(qh[:, MLA_NOPE:]) * q_scale).astype(BF16)
        k_ref[0, h, :, :MLA_NOPE] = k_all[:, h * MLA_NOPE:(h + 1) * MLA_NOPE].astype(BF16)
        k_ref[0, h, :, MLA_NOPE:] = k_pe
        v_ref[0, h, :, :MLA_V] = v_all[:, h * MLA_V:(h + 1) * MLA_V].astype(BF16)
        v_ref[0, h, :, MLA_V:] = ones_col


def _mla_prep(mla, cc, ss, qn_g, kvn_g, wq_ext, wk2, wv2, B, S, tm):
    tm = min(tm, S)
    ns = S // tm
    H = MLA_HEADS
    q_scale = (MLA_NOPE + MLA_ROPE) ** -0.5 * LOG2E
    full = lambda shape: pl.BlockSpec(shape, lambda b, i: (0,) * len(shape))
    return pl.pallas_call(
        functools.partial(_mla_prep_kernel, q_scale=q_scale),
        grid=(B, ns),
        in_specs=[pl.BlockSpec((tm, MLA_SEG), lambda b, i: (b * ns + i, 0)),
                  pl.BlockSpec((tm, 128), lambda b, i: (i, 0)),
                  pl.BlockSpec((tm, 128), lambda b, i: (i, 0)),
                  full((1, Q_LORA)), full((1, KV_LORA)),
                  full(wq_ext.shape), full(wk2.shape), full(wv2.shape)],
        out_specs=[pl.BlockSpec((1, H, tm, MLA_QK), lambda b, i: (b, 0, i, 0)),
                   pl.BlockSpec((1, H, tm, MLA_QK), lambda b, i: (b, 0, i, 0)),
                   pl.BlockSpec((1, H, tm, 2 * MLA_V), lambda b, i: (b, 0, i, 0))],
        out_shape=[jax.ShapeDtypeStruct((B, H, S, MLA_QK), BF16),
                   jax.ShapeDtypeStruct((B, H, S, MLA_QK), BF16),
                   jax.ShapeDtypeStruct((B, H, S, 2 * MLA_V), BF16)],
        compiler_params=_params("parallel", "parallel"),
        name="mla_prep",
    )(mla, cc, ss, qn_g, kvn_g, wq_ext, wk2, wv2)


def _attention_kernel(q_ref, k_ref, v_ref, o_ref, *, n_sub):
    tq = q_ref.shape[2]
    ts = tq // n_sub
    k = k_ref[0, 0]
    v = v_ref[0, 0]
    for u in range(n_sub):
        rows = slice(u * ts, (u + 1) * ts)
        s = _nt_dot(q_ref[0, 0, rows, :], k)
        m = jnp.max(s, -1, keepdims=True)
        p = jnp.exp2(s - m)
        o = _dot(p.astype(BF16), v)
        o_ref[0, rows, :] = (o[:, :MLA_V] / o[:, MLA_V:MLA_V + 1]).astype(o_ref.dtype)


def _attention(q, k, v, score_bytes):
    B, H, S, _ = q.shape
    ts = min(ATTN_ROW_GROUP, S)
    n_sub = max(min(score_bytes // (ts * S * 4), S // ts), 1)
    tq = ts * n_sub
    return pl.pallas_call(
        functools.partial(_attention_kernel, n_sub=n_sub),
        grid=(B, H, S // tq),
        in_specs=[pl.BlockSpec((1, 1, tq, MLA_QK), lambda b, h, i: (b, h, i, 0)),
                  pl.BlockSpec((1, 1, S, MLA_QK), lambda b, h, i: (b, h, 0, 0)),
                  pl.BlockSpec((1, 1, S, 2 * MLA_V), lambda b, h, i: (b, h, 0, 0))],
        out_specs=pl.BlockSpec((1, tq, MLA_V), lambda b, h, i: (b, i, h)),
        out_shape=jax.ShapeDtypeStruct((B, S, H * MLA_V), BF16),
        compiler_params=_params("parallel", "parallel", "arbitrary"),
        name="attention",
    )(q, k, v)


def _layer_norm(x, g, b):
    mu = jnp.mean(x, -1, keepdims=True)
    d = x - mu
    var = jnp.mean(d * d, -1, keepdims=True)
    return d * lax.rsqrt(var + LN_EPS) * g + b


def _merge_kernel(x_ref, yr_ref, om_ref, gate_ref, wro_ref, wmo_ref, wout_ref, g1_ref, b1_ref,
                  rwh_ref, rwc_ref, rb_ref, h_ref, hp_ref, te_ref, tw_ref, *, n_sub):
    ts = x_ref.shape[0] // n_sub
    for u in range(n_sub):
        r = slice(u * ts, (u + 1) * ts)
        y_a = _dot(yr_ref[r, :], wro_ref[...])
        y_b = _dot(om_ref[r, :], wmo_ref[...])
        gates = gate_ref[r, :].astype(F32)
        merged = (jax.nn.sigmoid(gates[:, :D_MODEL]) * y_a + jax.nn.sigmoid(gates[:, D_MODEL:]) * y_b)
        mix = _dot(merged.astype(BF16), wout_ref[...])
        h = _layer_norm(DEEPNORM_ALPHA * x_ref[r, :] + mix, g1_ref[...], b1_ref[...])
        h_ref[r, :] = h
        hp_ref[r, :] = _pack_bf16_pair(h)

        h_hi = h.astype(BF16)
        h_lo = (h - h_hi.astype(F32)).astype(BF16)
        both = _dot(h_hi, rwc_ref[...])
        logits = both[:, :N_EXPERTS] + both[:, N_EXPERTS:] + _dot(h_lo, rwh_ref[...]) + rb_ref[...]
        lane = lax.broadcasted_iota(I32, logits.shape, 1)
        vals, idxs = [], []
        for _ in range(TOP_K):
            m = jnp.max(logits, -1, keepdims=True)
            idx = jnp.min(jnp.where(logits == m, lane, N_EXPERTS), -1, keepdims=True)
            vals.append(m)
            idxs.append(idx)
            logits = jnp.where(lane == idx, -jnp.inf, logits)
        e = [jnp.exp(v - vals[0]) for v in vals]
        tot = e[0] + e[1] + e[2] + e[3]
        te_ref[r, :] = jnp.concatenate(idxs, -1)
        tw_ref[r, :] = jnp.concatenate([ek / tot for ek in e], -1)


def _merge(x, y_ret, o_mla, vg, w_ret_o, w_mla_o, w_out, ln_g, ln_b, rw_hi, rw_cat, rb, tm):
    T = x.shape[0]
    tm = min(tm, T)
    gate_blk = (2 * RET_V_W) // (2 * D_MODEL)
    full = lambda a: pl.BlockSpec(a.shape, lambda i: (0,) * a.ndim)
    row = lambda w: pl.BlockSpec((tm, w), lambda i: (i, 0))
    return pl.pallas_call(
        functools.partial(_merge_kernel, n_sub=1),
        grid=(T // tm,),
        in_specs=[row(D_MODEL), row(RET_V_W), row(D_MODEL),
                  pl.BlockSpec((tm, 2 * D_MODEL), lambda i: (i, gate_blk)),
                  full(w_ret_o), full(w_mla_o), full(w_out), full(ln_g), full(ln_b),
                  full(rw_hi), full(rw_cat), full(rb)],
        out_specs=[row(D_MODEL), row(HALF), row(TOP_K), row(TOP_K)],
        out_shape=[jax.ShapeDtypeStruct((T, D_MODEL), F32),
                   jax.ShapeDtypeStruct((T, HALF), I32),
                   jax.ShapeDtypeStruct((T, TOP_K), I32),
                   jax.ShapeDtypeStruct((T, TOP_K), F32)],
        compiler_params=_params("parallel"),
        name="merge_ln1_router",
    )(x, y_ret, o_mla, vg, w_ret_o, w_mla_o, w_out, ln_g, ln_b, rw_hi, rw_cat, rb)


def _rank_kernel(te_ref, rank_ref, cnt_ref, acc_ref):
    @pl.when(pl.program_id(0) == 0)
    def _():
        acc_ref[...] = jnp.zeros_like(acc_ref)

    tm = te_ref.shape[0]
    te = te_ref[...]
    lane = lax.broadcasted_iota(I32, (tm, N_EXPERTS), 1)
    earlier = (lax.broadcasted_iota(I32, (tm, tm), 0) > lax.broadcasted_iota(I32, (tm, tm), 1))
    earlier = jnp.where(earlier, 1.0, 0.0).astype(BF16)
    base = acc_ref[...]
    ranks = []
    for k in range(TOP_K):
        onehot = jnp.where(lane == te[:, k:k + 1], 1.0, 0.0)
        before = _dot(earlier, onehot.astype(BF16))
        ranks.append(jnp.sum(onehot * (before + base), -1, keepdims=True))
        base = base + jnp.sum(onehot, 0, keepdims=True)
    acc_ref[...] = base
    rank_ref[...] = jnp.concatenate(ranks, -1).astype(I32)
    cnt_ref[...] = base.astype(I32)


def _rank(top_e, tm):
    T = top_e.shape[0]
    tm = min(tm, T)
    return pl.pallas_call(
        _rank_kernel,
        grid=(T // tm,),
        in_specs=[pl.BlockSpec((tm, TOP_K), lambda i: (i, 0))],
        out_specs=[pl.BlockSpec((tm, TOP_K), lambda i: (i, 0)),
                   pl.BlockSpec((1, N_EXPERTS), lambda i: (0, 0))],
        out_shape=[jax.ShapeDtypeStruct((T, TOP_K), I32),
                   jax.ShapeDtypeStruct((1, N_EXPERTS), I32)],
        scratch_shapes=[pltpu.VMEM((1, N_EXPERTS), F32)],
        compiler_params=_params("arbitrary"),
        name="moe_rank",
    )(top_e)


def _sc_mesh():
    return plsc.VectorSubcoreMesh(core_axis_name="c", subcore_axis_name="s",
                                  num_cores=V7X_SC_CORES, num_subcores=V7X_SC_SUBCORES)


def _sc_worker():
    return lax.axis_index("s") * V7X_SC_CORES + lax.axis_index("c")


def _sc_dispatch_kernel(hp_hbm, dest_hbm, pad_hbm, zeros_hbm, xs_hbm, idx_v, rows_v, sem,
                        *, chunks_per_worker, pad_chunks_per_worker):
    R = SC_GATHER_ROWS
    worker = _sc_worker()

    @pl.loop(0, chunks_per_worker)
    def _(j):
        chunk = worker * chunks_per_worker + j
        pltpu.sync_copy(hp_hbm.at[pl.ds(chunk * R, R)], rows_v)
        pltpu.sync_copy(dest_hbm.at[chunk], idx_v)
        copies = [pltpu.async_copy(rows_v, xs_hbm.at[idx_v.at[k]], sem) for k in range(TOP_K)]
        for cp in copies:
            cp.wait()

    pltpu.sync_copy(zeros_hbm, rows_v)

    @pl.loop(0, pad_chunks_per_worker)
    def _(j):
        chunk = worker * pad_chunks_per_worker + j
        pltpu.sync_copy(pad_hbm.at[chunk], idx_v.at[0])
        pltpu.async_copy(rows_v, xs_hbm.at[idx_v.at[0]], sem).wait()


def _sc_dispatch(hp, dest_chunks, pad_chunks, n_rows):
    R = SC_GATHER_ROWS
    workers = V7X_SC_CORES * V7X_SC_SUBCORES
    n_chunks, n_pad_chunks = dest_chunks.shape[0], pad_chunks.shape[0]
    assert n_chunks % workers == 0 and n_pad_chunks % workers == 0
    return pl.kernel(
        functools.partial(_sc_dispatch_kernel, chunks_per_worker=n_chunks // workers,
                          pad_chunks_per_worker=n_pad_chunks // workers),
        out_type=jax.ShapeDtypeStruct((n_rows, HALF), I32),
        mesh=_sc_mesh(),
        scratch_types=[pltpu.VMEM((TOP_K, R), I32), pltpu.VMEM((R, HALF), I32), pltpu.SemaphoreType.DMA],
        name="moe_dispatch_sc",
    )(hp, dest_chunks, pad_chunks, jnp.zeros((R, HALF), I32))


def _expert_kernel(be_ref, nv_ref, x_ref, wg_ref, bg_ref, wu_ref, bu_ref, wd_ref, bd_ref, y_ref,
                   wgb_ref, wub_ref, wdb_ref):
    j = pl.program_id(0)
    valid = j < nv_ref[0]
    new_expert = jnp.logical_or(j == 0, be_ref[j] != be_ref[jnp.maximum(j - 1, 0)])

    @pl.when(jnp.logical_and(valid, new_expert))
    def _():
        wgb_ref[...] = wg_ref[0].astype(BF16)
        wub_ref[...] = wu_ref[0].astype(BF16)
        wdb_ref[...] = wd_ref[0].astype(BF16)

    @pl.when(valid)
    def _():
        lo, hi = _unpack_bf16_pair(x_ref[...])
        x = jnp.concatenate([lo, hi], -1).astype(BF16)
        gate = jnp.minimum(_dot(x, wgb_ref[...]) + bg_ref[0], SWIGLU_LIMIT)
        up = jnp.clip(_dot(x, wub_ref[...]) + bu_ref[0], -SWIGLU_LIMIT, SWIGLU_LIMIT)
        act = (up + 1.0) * (gate * jax.nn.sigmoid(SWIGLU_ALPHA * gate))
        y_ref[...] = _pack_bf16_pair(_dot(act.astype(BF16), wdb_ref[...]) + bd_ref[0])

    @pl.when(jnp.logical_not(valid))
    def _():
        y_ref[...] = jnp.zeros_like(y_ref)


def _experts(blk_e, n_valid, xs, wg, bg, wu, bu, wd, bd, tb):
    n_rows = xs.shape[0]
    wspec = pl.BlockSpec((1, D_MODEL, D_FF), lambda j, be, nv: (be[j], 0, 0))
    bspec = pl.BlockSpec((1, 1, D_FF), lambda j, be, nv: (be[j], 0, 0))
    return pl.pallas_call(
        _expert_kernel,
        grid_spec=pltpu.PrefetchScalarGridSpec(
            num_scalar_prefetch=2,
            grid=(n_rows // tb,),
            in_specs=[pl.BlockSpec((tb, HALF), lambda j, be, nv: (jnp.minimum(j, nv[0] - 1), 0)),
                      wspec, bspec, wspec, bspec, wspec, bspec],
            out_specs=pl.BlockSpec((tb, HALF), lambda j, be, nv: (j, 0)),
            scratch_shapes=[pltpu.VMEM((D_MODEL, D_FF), BF16), pltpu.VMEM((D_MODEL, D_FF), BF16),
                            pltpu.VMEM((D_FF, D_MODEL), BF16)],
        ),
        out_shape=jax.ShapeDtypeStruct((n_rows, HALF), I32),
        compiler_params=_params("arbitrary"),
        name="moe_experts",
    )(blk_e, n_valid, xs, wg, bg, wu, bu, wd, bd)


def _sc_gather_kernel(table_hbm, idx_hbm, out_hbm, idx_v, rows_v, sem, *, rows_per_worker):
    base = _sc_worker() * rows_per_worker

    @pl.loop(0, rows_per_worker // SC_GATHER_ROWS)
    def _(j):
        off = base + j * SC_GATHER_ROWS
        pltpu.sync_copy(idx_hbm.at[pl.ds(off, SC_GATHER_ROWS)], idx_v)
        pltpu.async_copy(table_hbm.at[idx_v], rows_v, sem).wait()
        pltpu.sync_copy(rows_v, out_hbm.at[pl.ds(off, SC_GATHER_ROWS)])


def _sc_gather_rows(table, idx):
    n, width = idx.shape[0], table.shape[1]
    workers = V7X_SC_CORES * V7X_SC_SUBCORES
    assert n % (workers * SC_GATHER_ROWS) == 0
    return pl.kernel(
        functools.partial(_sc_gather_kernel, rows_per_worker=n // workers),
        out_type=jax.ShapeDtypeStruct((n, width), table.dtype),
        mesh=_sc_mesh(),
        scratch_types=[pltpu.VMEM((SC_GATHER_ROWS,), I32), pltpu.VMEM((SC_GATHER_ROWS, width), table.dtype),
                       pltpu.SemaphoreType.DMA],
        name="moe_gather_sc",
    )(table, idx)


def _ln2_kernel(y4_ref, h_ref, tw_ref, g2_ref, b2_ref, o_ref):
    tw = tw_ref[...]
    f_lo = jnp.zeros(y4_ref.shape[1:], F32)
    f_hi = jnp.zeros(y4_ref.shape[1:], F32)
    for k in range(TOP_K):
        lo, hi = _unpack_bf16_pair(y4_ref[k])
        w = tw[:, k:k + 1]
        f_lo = f_lo + w * lo
        f_hi = f_hi + w * hi
    f = jnp.concatenate([f_lo, f_hi], -1)
    o_ref[...] = _layer_norm(DEEPNORM_ALPHA * h_ref[...] + f, g2_ref[...], b2_ref[...])


def _weighted_sum_ln2(y4, h, top_w, ln_g, ln_b, tm):
    T = h.shape[0]
    tm = min(tm, T)
    full = lambda a: pl.BlockSpec(a.shape, lambda i: (0,) * a.ndim)
    return pl.pallas_call(
        _ln2_kernel,
        grid=(T // tm,),
        in_specs=[pl.BlockSpec((TOP_K, tm, HALF), lambda i: (0, i, 0)),
                  pl.BlockSpec((tm, D_MODEL), lambda i: (i, 0)),
                  pl.BlockSpec((tm, TOP_K), lambda i: (i, 0)),
                  full(ln_g), full(ln_b)],
        out_specs=pl.BlockSpec((tm, D_MODEL), lambda i: (i, 0)),
        out_shape=jax.ShapeDtypeStruct((T, D_MODEL), F32),
        compiler_params=_params("parallel"),
        name="moe_sum_ln2",
    )(y4, h, top_w, ln_g, ln_b)


def _rope_table(seq, dim):
    inv = ROPE_THETA ** (-jnp.arange(0, dim, 2, dtype=F32) / dim)
    ang = jnp.arange(seq, dtype=F32)[:, None] * inv[None, :]
    return jnp.cos(ang), jnp.sin(ang)


def _prepare_weights(w_in, ret_decay_logit, ret_gn_g, w_ret_o, mla_q_norm_g, w_uq, mla_kv_norm_g,
                     w_uk, w_uv, w_mla_o, w_out, ln1_g, ln1_b, router_w, router_b, exp_w_gate,
                     exp_b_gate, exp_w_up, exp_b_up, exp_w_down, exp_b_down, ln2_g, ln2_b):
    o = 0
    seg = {}
    for name, width in (("q", RET_QK_W), ("k", RET_QK_W), ("v", RET_V_W), ("g", RET_V_W),
                        ("cq", Q_LORA), ("ckv", KV_LORA), ("kpe", MLA_ROPE),
                        ("ga", D_MODEL), ("gb", D_MODEL)):
        seg[name] = w_in[:, o:o + width]
        o += width
    half = MLA_ROPE // 2
    swap = lambda w: jnp.concatenate([w[:, half:], w[:, :half]], -1)
    w_qk = jnp.concatenate([seg["q"], seg["k"]], -1).astype(BF16)
    w_vg = jnp.concatenate([seg["v"], seg["g"], seg["ga"], seg["gb"]], -1).astype(BF16)
    w_mla = jnp.concatenate([seg["cq"], seg["ckv"], seg["kpe"], swap(seg["kpe"])], -1).astype(BF16)
    wq = w_uq.reshape(Q_LORA, MLA_HEADS, MLA_NOPE + MLA_ROPE)
    wq_pe = wq[:, :, MLA_NOPE:]
    wq_ext = jnp.concatenate([wq[:, :, :MLA_NOPE], wq_pe,
                              jnp.concatenate([wq_pe[:, :, half:], wq_pe[:, :, :half]], -1)], -1)
    rw_hi = router_w.astype(BF16)
    return dict(
        w_qk=w_qk, w_vg=w_vg, w_mla=w_mla,
        log_gamma=jax.nn.log_sigmoid(ret_decay_logit.astype(F32)),
        gn_g=ret_gn_g.reshape(1, RET_V_W),
        w_ret_o=w_ret_o.astype(BF16),
        qn_g=mla_q_norm_g.reshape(1, Q_LORA), kvn_g=mla_kv_norm_g.reshape(1, KV_LORA),
        wq_ext=wq_ext.reshape(Q_LORA, MLA_HEADS * MLA_QK).astype(BF16),
        wk2=w_uk.reshape(KV_LORA, MLA_HEADS * MLA_NOPE).astype(BF16),
        wv2=w_uv.reshape(KV_LORA, MLA_HEADS * MLA_V).astype(BF16),
        w_mla_o=w_mla_o.astype(BF16), w_out=w_out.astype(BF16),
        ln1_g=ln1_g.reshape(1, D_MODEL), ln1_b=ln1_b.reshape(1, D_MODEL),
        rw_hi=rw_hi,
        rw_cat=jnp.concatenate([rw_hi, (router_w - rw_hi.astype(F32)).astype(BF16)], -1),
        rb=router_b.reshape(1, N_EXPERTS),
        wg=exp_w_gate, bg=exp_b_gate.reshape(N_EXPERTS, 1, D_FF),
        wu=exp_w_up, bu=exp_b_up.reshape(N_EXPERTS, 1, D_FF),
        wd=exp_w_down, bd=exp_b_down.reshape(N_EXPERTS, 1, D_MODEL),
        ln2_g=ln2_g.reshape(1, D_MODEL), ln2_b=ln2_b.reshape(1, D_MODEL),
    )


def _encoder_layer(x3, p):
    B, S, D = x3.shape
    T = B * S
    x = x3.reshape(T, D)

    cos_r, sin_r = _rope_table(S, RET_DK)
    cos_m, sin_m = _rope_table(S, MLA_ROPE)
    zeros = jnp.zeros((S, 128 - MLA_ROPE), F32)
    cc = jnp.concatenate([cos_m, cos_m, zeros], -1)
    ss = jnp.concatenate([-sin_m, sin_m, zeros], -1)

    qk = _qk_rope_proj(x, p["w_qk"], cos_r, sin_r, S, tm=2048)
    vg = _linear(x, p["w_vg"], BF16, tm=1024, tn=1024, name="proj_vg")
    mla = _linear(x, p["w_mla"], F32, tm=1024, tn=MLA_SEG, name="proj_mla")

    y_ret = _retention(p["log_gamma"], qk, vg, p["gn_g"], B, S, chunk=256).reshape(T, RET_V_W)
    q, k, v = _mla_prep(mla, cc, ss, p["qn_g"], p["kvn_g"], p["wq_ext"], p["wk2"], p["wv2"], B, S, tm=512)
    o_mla = _attention(q, k, v, ATTN_SCORE_BYTES).reshape(T, MLA_HEADS * MLA_V)

    h, hp, top_e, top_w = _merge(x, y_ret, o_mla, vg, p["w_ret_o"], p["w_mla_o"], p["w_out"],
                                 p["ln1_g"], p["ln1_b"], p["rw_hi"], p["rw_cat"], p["rb"], tm=512)

    rank, counts = _rank(top_e, tm=512)
    tb = EXPERT_BLOCK
    n_rows = T * TOP_K + N_EXPERTS * tb
    counts = counts.reshape(N_EXPERTS)
    padded = (counts + tb - 1) // tb * tb
    pend = jnp.cumsum(padded)
    pstart = pend - padded
    dest = pstart[top_e] + rank
    blk_start = jnp.arange(n_rows // tb, dtype=I32) * tb
    blk_e = jnp.minimum(jnp.sum((pend[None, :] <= blk_start[:, None]).astype(I32), -1), N_EXPERTS - 1)
    n_valid = (pend[-1:] // tb).astype(I32)

    seg = jnp.concatenate([padded - counts, n_rows - pend[-1:]])
    seg_end = jnp.cumsum(seg)
    seg_base = jnp.concatenate([pstart + counts, pend[-1:]])
    j = jnp.arange(n_rows - T * TOP_K, dtype=I32)
    s = jnp.sum((seg_end[None, :] <= j[:, None]).astype(I32), -1)
    pad_rows = (seg_base[s] + j - (seg_end - seg)[s]).astype(I32)

    R = SC_GATHER_ROWS
    dest_chunks = dest.reshape(T // R, R, TOP_K).transpose(0, 2, 1)
    xs = _sc_dispatch(hp, dest_chunks, pad_rows.reshape(-1, R), n_rows)
    ys = _experts(blk_e, n_valid, xs, p["wg"], p["bg"], p["wu"], p["bu"], p["wd"], p["bd"], tb)
    y4 = _sc_gather_rows(ys, dest.T.reshape(T * TOP_K)).reshape(TOP_K, T, HALF)
    out = _weighted_sum_ln2(y4, h, top_w, p["ln2_g"], p["ln2_b"], tm=512)
    return out.reshape(B, S, D)


def kernel(x_prompt, x_sample, w_in, ret_decay_logit, ret_gn_g, w_ret_o, mla_q_norm_g, w_uq, mla_kv_norm_g, w_uk, w_uv, w_mla_o, w_out, ln1_g, ln1_b, router_w, router_b, exp_w_gate, exp_b_gate, exp_w_up, exp_b_up, exp_w_down, exp_b_down, ln2_g, ln2_b):
    params = (w_in, ret_decay_logit, ret_gn_g, w_ret_o, mla_q_norm_g, w_uq, mla_kv_norm_g, w_uk, w_uv,
              w_mla_o, w_out, ln1_g, ln1_b, router_w, router_b, exp_w_gate, exp_b_gate, exp_w_up,
              exp_b_up, exp_w_down, exp_b_down, ln2_g, ln2_b)
    depth = w_in.shape[0]
    y_prompt, y_sample = x_prompt, x_sample
    for l in range(depth):
        p = _prepare_weights(*[w[l] for w in params])
        y_prompt = _encoder_layer(y_prompt, p)
        y_sample = _encoder_layer(y_sample, p)
    return (y_prompt, y_sample)
```

```python
import functools

import jax
import jax.numpy as jnp
from jax import lax
from jax.experimental import pallas as pl
from jax.experimental.pallas import tpu as pltpu
from jax.experimental.pallas import tpu_sc as plsc

F32 = jnp.float32
BF16 = jnp.bfloat16
U32 = jnp.uint32
I32 = jnp.int32

D_MODEL = 1024
RET_HEADS = 4
RET_DK = 256
RET_DV = 512
RET_CHUNK = 128
MLA_HEADS = 8
MLA_NOPE = 128
MLA_ROPE = 64
MLA_V = 128
Q_LORA = 384
KV_LORA = 256
ROPE_THETA = 10000.0
N_EXPERTS = 32
TOP_K = 4
D_FF = 1024
SWIGLU_LIMIT = 7.0
SWIGLU_ALPHA = 1.702
LN_EPS = 1e-5
RMS_EPS = 1e-6
DEEPNORM_ALPHA = 2.0 ** 0.25
RET_QK_W = RET_HEADS * RET_DK
RET_V_W = RET_HEADS * RET_DV
MLA_SEG = Q_LORA + KV_LORA + 2 * MLA_ROPE
MLA_QK = 256
LOG2E = 1.4426950408889634

V7X_VMEM_LIMIT = 56 * 1024 * 1024
V7X_SC_CORES = 2
V7X_SC_SUBCORES = 16
SC_GATHER_ROWS = 128
EXPERT_BLOCK = 512
ATTN_ROW_GROUP = 256
ATTN_SCORE_BYTES = 32 * 1024 * 1024
HALF = D_MODEL // 2


def _params(*sem):
    return pltpu.CompilerParams(dimension_semantics=sem, vmem_limit_bytes=V7X_VMEM_LIMIT)


def _nt_dot(a, b):
    return lax.dot_general(a, b, (((1,), (1,)), ((), ())), preferred_element_type=F32)


def _tn_dot(a, b):
    return lax.dot_general(a, b, (((0,), (0,)), ((), ())), preferred_element_type=F32)


def _dot(a, b):
    return jnp.dot(a, b, preferred_element_type=F32)


def _pack_bf16_pair(x):
    bits = lax.bitcast_convert_type(x.astype(BF16).astype(F32), U32)
    return lax.bitcast_convert_type((bits[:, :HALF] >> 16) | bits[:, HALF:], I32)


def _unpack_bf16_pair(p):
    p = lax.bitcast_convert_type(p, U32)
    lo = lax.bitcast_convert_type(p << 16, F32)
    hi = lax.bitcast_convert_type(p & jnp.uint32(0xFFFF0000), F32)
    return lo, hi


def _linear_kernel(x_ref, w_ref, o_ref, xb_ref):
    @pl.when(pl.program_id(1) == 0)
    def _():
        xb_ref[...] = x_ref[...].astype(BF16)

    o_ref[...] = _dot(xb_ref[...], w_ref[...]).astype(o_ref.dtype)


def _linear(x, w, out_dtype, tm, tn, name):
    M, K = x.shape
    N = w.shape[1]
    tm, tn = min(tm, M), min(tn, N)
    return pl.pallas_call(
        _linear_kernel,
        grid=(M // tm, N // tn),
        in_specs=[pl.BlockSpec((tm, K), lambda i, j: (i, 0)),
                  pl.BlockSpec((K, tn), lambda i, j: (0, j))],
        out_specs=pl.BlockSpec((tm, tn), lambda i, j: (i, j)),
        out_shape=jax.ShapeDtypeStruct((M, N), out_dtype),
        scratch_shapes=[pltpu.VMEM((tm, K), BF16)],
        compiler_params=_params("parallel", "arbitrary"),
        name=name,
    )(x, w)


def _qk_rope_kernel(x_ref, w_ref, cos_ref, sin_ref, o_ref, xb_ref, *, q_scale):
    j = pl.program_id(1)

    @pl.when(j == 0)
    def _():
        xb_ref[...] = x_ref[...].astype(BF16)

    y = _dot(xb_ref[...], w_ref[...])
    half = RET_DK // 2
    heads = y.shape[1] // RET_DK
    c, s = cos_ref[...], sin_ref[...]
    scale = jnp.where(j < RET_HEADS // heads, q_scale, 1.0).astype(F32)
    for h in range(heads):
        x1 = y[:, h * RET_DK:h * RET_DK + half]
        x2 = y[:, h * RET_DK + half:(h + 1) * RET_DK]
        o_ref[:, h * RET_DK:h * RET_DK + half] = ((x1 * c - x2 * s) * scale).astype(o_ref.dtype)
        o_ref[:, h * RET_DK + half:(h + 1) * RET_DK] = ((x1 * s + x2 * c) * scale).astype(o_ref.dtype)


def _qk_rope_proj(x, w_qk, cos, sin, seq, tm):
    M, K = x.shape
    N = w_qk.shape[1]
    tm = min(tm, seq)
    nseq = seq // tm
    tn = 2 * RET_DK
    return pl.pallas_call(
        functools.partial(_qk_rope_kernel, q_scale=RET_DK ** -0.5),
        grid=(M // tm, N // tn),
        in_specs=[pl.BlockSpec((tm, K), lambda i, j: (i, 0)),
                  pl.BlockSpec((K, tn), lambda i, j: (0, j)),
                  pl.BlockSpec((tm, RET_DK // 2), lambda i, j: (i % nseq, 0)),
                  pl.BlockSpec((tm, RET_DK // 2), lambda i, j: (i % nseq, 0))],
        out_specs=pl.BlockSpec((tm, tn), lambda i, j: (i, j)),
        out_shape=jax.ShapeDtypeStruct((M, N), BF16),
        scratch_shapes=[pltpu.VMEM((tm, K), BF16)],
        compiler_params=_params("parallel", "arbitrary"),
        name="qk_rope_proj",
    )(x, w_qk, cos, sin)


def _retention_kernel(lg_ref, q_ref, k_ref, v_ref, g_ref, gn_ref, o_ref, rf_ref, rb_ref, sf_ref, sb_ref,
                      *, chunk):
    h = pl.program_id(1)
    S = q_ref.shape[1]
    C = chunk
    NC = S // C
    lgf = lg_ref[0, h]
    lgb = lg_ref[1, h]

    n_col = lax.broadcasted_iota(I32, (C, 1), 0).astype(F32)
    diff = (lax.broadcasted_iota(I32, (C, C), 0) - lax.broadcasted_iota(I32, (C, C), 1)).astype(F32)
    decay = jnp.where(diff >= 0, jnp.exp(lgf * jnp.maximum(diff, 0.0)),
                      jnp.exp(lgb * jnp.maximum(-diff, 0.0)))
    xi_f = jnp.exp(lgf * (n_col + 1.0))
    zeta_f = jnp.exp(lgf * (C - 1.0 - n_col))
    xi_b = jnp.exp(lgb * (C - n_col))
    zeta_b = jnp.exp(lgb * n_col)
    ones = jnp.ones((1, 1), F32)
    g_f = jnp.exp(ones * (lgf * C))
    g_b = jnp.exp(ones * (lgb * C))

    def rows(i):
        return pl.ds(pl.multiple_of(i * C, C), C)

    sf_ref[...] = jnp.zeros_like(sf_ref)
    sb_ref[...] = jnp.zeros_like(sb_ref)

    def scan(j, carry):
        i_f, i_b = j, NC - 1 - j
        rf_ref[i_f] = sf_ref[...].astype(BF16)
        rb_ref[i_b] = sb_ref[...].astype(BF16)
        kf = (k_ref[0, rows(i_f), :].astype(F32) * zeta_f).astype(BF16)
        kb = (k_ref[0, rows(i_b), :].astype(F32) * zeta_b).astype(BF16)
        sf_ref[...] = sf_ref[...] * g_f + _tn_dot(kf, v_ref[0, rows(i_f), :])
        sb_ref[...] = sb_ref[...] * g_b + _tn_dot(kb, v_ref[0, rows(i_b), :])
        return carry

    lax.fori_loop(0, NC, scan, 0, unroll=4 if NC % 4 == 0 else 1)

    gn = gn_ref[...]

    def emit(i, carry):
        r = rows(i)
        qb = q_ref[0, r, :]
        q = qb.astype(F32)
        v = v_ref[0, r, :]
        inner = _nt_dot(qb, k_ref[0, r, :]) * decay
        o = (_dot(inner.astype(BF16), v)
             + _dot((q * xi_f).astype(BF16), rf_ref[i])
             + _dot((q * xi_b).astype(BF16), rb_ref[i]))
        mu = jnp.mean(o, -1, keepdims=True)
        d = o - mu
        var = jnp.mean(d * d, -1, keepdims=True)
        on = d * lax.rsqrt(var + LN_EPS) * gn
        gate = g_ref[0, r, :].astype(F32)
        o_ref[0, r, :] = (gate * jax.nn.sigmoid(gate) * on).astype(o_ref.dtype)
        return carry

    lax.fori_loop(0, NC, emit, 0, unroll=8 if NC % 8 == 0 else 1)


def _retention(log_gamma, qk, vg, gn_g, B, S, chunk):
    qk3 = qk.reshape(B, S, 2 * RET_QK_W)
    vg3 = vg.reshape(B, S, vg.shape[1])
    H = RET_HEADS
    chunk = min(chunk, S)
    nc = S // chunk
    return pl.pallas_call(
        functools.partial(_retention_kernel, chunk=chunk),
        grid=(B, H),
        in_specs=[pl.BlockSpec(memory_space=pltpu.SMEM),
                  pl.BlockSpec((1, S, RET_DK), lambda b, h: (b, 0, h)),
                  pl.BlockSpec((1, S, RET_DK), lambda b, h: (b, 0, H + h)),
                  pl.BlockSpec((1, S, RET_DV), lambda b, h: (b, 0, h)),
                  pl.BlockSpec((1, S, RET_DV), lambda b, h: (b, 0, H + h)),
                  pl.BlockSpec((1, RET_DV), lambda b, h: (0, h))],
        out_specs=pl.BlockSpec((1, S, RET_DV), lambda b, h: (b, 0, h)),
        out_shape=jax.ShapeDtypeStruct((B, S, RET_V_W), BF16),
        scratch_shapes=[pltpu.VMEM((nc, RET_DK, RET_DV), BF16), pltpu.VMEM((nc, RET_DK, RET_DV), BF16),
                        pltpu.VMEM((RET_DK, RET_DV), F32), pltpu.VMEM((RET_DK, RET_DV), F32)],
        compiler_params=_params("parallel", "arbitrary"),
        name="retention",
    )(log_gamma, qk3, qk3, vg3, vg3, gn_g)


def _rms(x, g):
    return x * lax.rsqrt(jnp.mean(x * x, -1, keepdims=True) + RMS_EPS) * g


def _mla_prep_kernel(m_ref, cc_ref, ss_ref, qg_ref, kg_ref, wq_ref, wk_ref, wv_ref,
                     q_ref, k_ref, v_ref, *, q_scale):
    x = m_ref[...]
    cc, ss = cc_ref[...], ss_ref[...]

    def rope(pair):
        return pair * cc + pltpu.roll(pair, MLA_ROPE, 1) * ss

    c_q = _rms(x[:, :Q_LORA], qg_ref[...]).astype(BF16)
    c_kv = _rms(x[:, Q_LORA:Q_LORA + KV_LORA], kg_ref[...]).astype(BF16)
    k_pe = rope(x[:, Q_LORA + KV_LORA:]).astype(BF16)
    q_all = _dot(c_q, wq_ref[...])
    k_all = _dot(c_kv, wk_ref[...])
    v_all = _dot(c_kv, wv_ref[...])
    lane = lax.broadcasted_iota(I32, (x.shape[0], MLA_V), 1)
    ones_col = jnp.where(lane == 0, 1.0, 0.0).astype(BF16)
    for h in range(MLA_HEADS):
        qh = q_all[:, h * MLA_QK:(h + 1) * MLA_QK]
        q_ref[0, h, :, :MLA_NOPE] = (qh[:, :MLA_NOPE] * q_scale).astype(BF16)
        q_ref[0, h, :, MLA_NOPE:] = (rope(qh[:, MLA_NOPE:]) * q_scale).astype(BF16)
        k_ref[0, h, :, :MLA_NOPE] = k_all[:, h * MLA_NOPE:(h + 1) * MLA_NOPE].astype(BF16)
        k_ref[0, h, :, MLA_NOPE:] = k_pe
        v_ref[0, h, :, :MLA_V] = v_all[:, h * MLA_V:(h + 1) * MLA_V].astype(BF16)
        v_ref[0, h, :, MLA_V:] = ones_col


def _mla_prep(mla, cc, ss, qn_g, kvn_g, wq_ext, wk2, wv2, B, S, tm):
    tm = min(tm, S)
    ns = S // tm
    H = MLA_HEADS
    q_scale = (MLA_NOPE + MLA_ROPE) ** -0.5 * LOG2E
    full = lambda shape: pl.BlockSpec(shape, lambda b, i: (0,) * len(shape))
    return pl.pallas_call(
        functools.partial(_mla_prep_kernel, q_scale=q_scale),
        grid=(B, ns),
        in_specs=[pl.BlockSpec((tm, MLA_SEG), lambda b, i: (b * ns + i, 0)),
                  pl.BlockSpec((tm, 128), lambda b, i: (i, 0)),
                  pl.BlockSpec((tm, 128), lambda b, i: (i, 0)),
                  full((1, Q_LORA)), full((1, KV_LORA)),
                  full(wq_ext.shape), full(wk2.shape), full(wv2.shape)],
        out_specs=[pl.BlockSpec((1, H, tm, MLA_QK), lambda b, i: (b, 0, i, 0)),
                   pl.BlockSpec((1, H, tm, MLA_QK), lambda b, i: (b, 0, i, 0)),
                   pl.BlockSpec((1, H, tm, 2 * MLA_V), lambda b, i: (b, 0, i, 0))],
        out_shape=[jax.ShapeDtypeStruct((B, H, S, MLA_QK), BF16),
                   jax.ShapeDtypeStruct((B, H, S, MLA_QK), BF16),
                   jax.ShapeDtypeStruct((B, H, S, 2 * MLA_V), BF16)],
        compiler_params=_params("parallel", "parallel"),
        name="mla_prep",
    )(mla, cc, ss, qn_g, kvn_g, wq_ext, wk2, wv2)


def _attention_kernel(q_ref, k_ref, v_ref, o_ref, *, n_sub):
    tq = q_ref.shape[2]
    ts = tq // n_sub
    k = k_ref[0, 0]
    v = v_ref[0, 0]
    for u in range(n_sub):
        rows = slice(u * ts, (u + 1) * ts)
        s = _nt_dot(q_ref[0, 0, rows, :], k)
        m = jnp.max(s, -1, keepdims=True)
        p = jnp.exp2(s - m)
        o = _dot(p.astype(BF16), v)
        o_ref[0, rows, :] = (o[:, :MLA_V] / o[:, MLA_V:MLA_V + 1]).astype(o_ref.dtype)


def _attention(q, k, v, score_bytes):
    B, H, S, _ = q.shape
    ts = min(ATTN_ROW_GROUP, S)
    n_sub = max(min(score_bytes // (ts * S * 4), S // ts), 1)
    tq = ts * n_sub
    return pl.pallas_call(
        functools.partial(_attention_kernel, n_sub=n_sub),
        grid=(B, H, S // tq),
        in_specs=[pl.BlockSpec((1, 1, tq, MLA_QK), lambda b, h, i: (b, h, i, 0)),
                  pl.BlockSpec((1, 1, S, MLA_QK), lambda b, h, i: (b, h, 0, 0)),
                  pl.BlockSpec((1, 1, S, 2 * MLA_V), lambda b, h, i: (b, h, 0, 0))],
        out_specs=pl.BlockSpec((1, tq, MLA_V), lambda b, h, i: (b, i, h)),
        out_shape=jax.ShapeDtypeStruct((B, S, H * MLA_V), BF16),
        compiler_params=_params("parallel", "parallel", "arbitrary"),
        name="attention",
    )(q, k, v)


def _layer_norm(x, g, b):
    mu = jnp.mean(x, -1, keepdims=True)
    d = x - mu
    var = jnp.mean(d * d, -1, keepdims=True)
    return d * lax.rsqrt(var + LN_EPS) * g + b


def _merge_kernel(x_ref, yr_ref, om_ref, gate_ref, wro_ref, wmo_ref, wout_ref, g1_ref, b1_ref,
                  rwh_ref, rwc_ref, rb_ref, h_ref, hp_ref, te_ref, tw_ref, *, n_sub):
    ts = x_ref.shape[0] // n_sub
    for u in range(n_sub):
        r = slice(u * ts, (u + 1) * ts)
        y_a = _dot(yr_ref[r, :], wro_ref[...])
        y_b = _dot(om_ref[r, :], wmo_ref[...])
        gates = gate_ref[r, :].astype(F32)
        merged = (jax.nn.sigmoid(gates[:, :D_MODEL]) * y_a + jax.nn.sigmoid(gates[:, D_MODEL:]) * y_b)
        mix = _dot(merged.astype(BF16), wout_ref[...])
        h = _layer_norm(DEEPNORM_ALPHA * x_ref[r, :] + mix, g1_ref[...], b1_ref[...])
        h_ref[r, :] = h
        hp_ref[r, :] = _pack_bf16_pair(h)

        h_hi = h.astype(BF16)
        h_lo = (h - h_hi.astype(F32)).astype(BF16)
        both = _dot(h_hi, rwc_ref[...])
        logits = both[:, :N_EXPERTS] + both[:, N_EXPERTS:] + _dot(h_lo, rwh_ref[...]) + rb_ref[...]
        lane = lax.broadcasted_iota(I32, logits.shape, 1)
        vals, idxs = [], []
        for _ in range(TOP_K):
            m = jnp.max(logits, -1, keepdims=True)
            idx = jnp.min(jnp.where(logits == m, lane, N_EXPERTS), -1, keepdims=True)
            vals.append(m)
            idxs.append(idx)
            logits = jnp.where(lane == idx, -jnp.inf, logits)
        e = [jnp.exp(v - vals[0]) for v in vals]
        tot = e[0] + e[1] + e[2] + e[3]
        te_ref[r, :] = jnp.concatenate(idxs, -1)
        tw_ref[r, :] = jnp.concatenate([ek / tot for ek in e], -1)


def _merge(x, y_ret, o_mla, vg, w_ret_o, w_mla_o, w_out, ln_g, ln_b, rw_hi, rw_cat, rb, tm):
    T = x.shape[0]
    tm = min(tm, T)
    gate_blk = (2 * RET_V_W) // (2 * D_MODEL)
    full = lambda a: pl.BlockSpec(a.shape, lambda i: (0,) * a.ndim)
    row = lambda w: pl.BlockSpec((tm, w), lambda i: (i, 0))
    return pl.pallas_call(
        functools.partial(_merge_kernel, n_sub=1),
        grid=(T // tm,),
        in_specs=[row(D_MODEL), row(RET_V_W), row(D_MODEL),
                  pl.BlockSpec((tm, 2 * D_MODEL), lambda i: (i, gate_blk)),
                  full(w_ret_o), full(w_mla_o), full(w_out), full(ln_g), full(ln_b),
                  full(rw_hi), full(rw_cat), full(rb)],
        out_specs=[row(D_MODEL), row(HALF), row(TOP_K), row(TOP_K)],
        out_shape=[jax.ShapeDtypeStruct((T, D_MODEL), F32),
                   jax.ShapeDtypeStruct((T, HALF), I32),
                   jax.ShapeDtypeStruct((T, TOP_K), I32),
                   jax.ShapeDtypeStruct((T, TOP_K), F32)],
        compiler_params=_params("parallel"),
        name="merge_ln1_router",
    )(x, y_ret, o_mla, vg, w_ret_o, w_mla_o, w_out, ln_g, ln_b, rw_hi, rw_cat, rb)


def _rank_kernel(te_ref, rank_ref, cnt_ref, acc_ref):
    @pl.when(pl.program_id(0) == 0)
    def _():
        acc_ref[...] = jnp.zeros_like(acc_ref)

    tm = te_ref.shape[0]
    te = te_ref[...]
    lane = lax.broadcasted_iota(I32, (tm, N_EXPERTS), 1)
    earlier = (lax.broadcasted_iota(I32, (tm, tm), 0) > lax.broadcasted_iota(I32, (tm, tm), 1))
    earlier = jnp.where(earlier, 1.0, 0.0).astype(BF16)
    base = acc_ref[...]
    ranks = []
    for k in range(TOP_K):
        onehot = jnp.where(lane == te[:, k:k + 1], 1.0, 0.0)
        before = _dot(earlier, onehot.astype(BF16))
        ranks.append(jnp.sum(onehot * (before + base), -1, keepdims=True))
        base = base + jnp.sum(onehot, 0, keepdims=True)
    acc_ref[...] = base
    rank_ref[...] = jnp.concatenate(ranks, -1).astype(I32)
    cnt_ref[...] = base.astype(I32)


def _rank(top_e, tm):
    T = top_e.shape[0]
    tm = min(tm, T)
    return pl.pallas_call(
        _rank_kernel,
        grid=(T // tm,),
        in_specs=[pl.BlockSpec((tm, TOP_K), lambda i: (i, 0))],
        out_specs=[pl.BlockSpec((tm, TOP_K), lambda i: (i, 0)),
                   pl.BlockSpec((1, N_EXPERTS), lambda i: (0, 0))],
        out_shape=[jax.ShapeDtypeStruct((T, TOP_K), I32),
                   jax.ShapeDtypeStruct((1, N_EXPERTS), I32)],
        scratch_shapes=[pltpu.VMEM((1, N_EXPERTS), F32)],
        compiler_params=_params("arbitrary"),
        name="moe_rank",
    )(top_e)


def _sc_mesh():
    return plsc.VectorSubcoreMesh(core_axis_name="c", subcore_axis_name="s",
                                  num_cores=V7X_SC_CORES, num_subcores=V7X_SC_SUBCORES)


def _sc_worker():
    return lax.axis_index("s") * V7X_SC_CORES + lax.axis_index("c")


def _sc_dispatch_kernel(hp_hbm, dest_hbm, pad_hbm, zeros_hbm, xs_hbm, idx_v, rows_v, sem,
                        *, chunks_per_worker, pad_chunks_per_worker):
    R = SC_GATHER_ROWS
    worker = _sc_worker()

    @pl.loop(0, chunks_per_worker)
    def _(j):
        chunk = worker * chunks_per_worker + j
        pltpu.sync_copy(hp_hbm.at[pl.ds(chunk * R, R)], rows_v)
        pltpu.sync_copy(dest_hbm.at[chunk], idx_v)
        copies = [pltpu.async_copy(rows_v, xs_hbm.at[idx_v.at[k]], sem) for k in range(TOP_K)]
        for cp in copies:
            cp.wait()

    pltpu.sync_copy(zeros_hbm, rows_v)

    @pl.loop(0, pad_chunks_per_worker)
    def _(j):
        chunk = worker * pad_chunks_per_worker + j
        pltpu.sync_copy(pad_hbm.at[chunk], idx_v.at[0])
        pltpu.async_copy(rows_v, xs_hbm.at[idx_v.at[0]], sem).wait()


def _sc_dispatch(hp, dest_chunks, pad_chunks, n_rows):
    R = SC_GATHER_ROWS
    workers = V7X_SC_CORES * V7X_SC_SUBCORES
    n_chunks, n_pad_chunks = dest_chunks.shape[0], pad_chunks.shape[0]
    assert n_chunks % workers == 0 and n_pad_chunks % workers == 0
    return pl.kernel(
        functools.partial(_sc_dispatch_kernel, chunks_per_worker=n_chunks // workers,
                          pad_chunks_per_worker=n_pad_chunks // workers),
        out_type=jax.ShapeDtypeStruct((n_rows, HALF), I32),
        mesh=_sc_mesh(),
        scratch_types=[pltpu.VMEM((TOP_K, R), I32), pltpu.VMEM((R, HALF), I32), pltpu.SemaphoreType.DMA],
        name="moe_dispatch_sc",
    )(hp, dest_chunks, pad_chunks, jnp.zeros((R, HALF), I32))


def _expert_kernel(be_ref, nv_ref, x_ref, wg_ref, bg_ref, wu_ref, bu_ref, wd_ref, bd_ref, y_ref,
                   wgb_ref, wub_ref, wdb_ref):
    j = pl.program_id(0)
    valid = j < nv_ref[0]
    new_expert = jnp.logical_or(j == 0, be_ref[j] != be_ref[jnp.maximum(j - 1, 0)])

    @pl.when(jnp.logical_and(valid, new_expert))
    def _():
        wgb_ref[...] = wg_ref[0].astype(BF16)
        wub_ref[...] = wu_ref[0].astype(BF16)
        wdb_ref[...] = wd_ref[0].astype(BF16)

    @pl.when(valid)
    def _():
        lo, hi = _unpack_bf16_pair(x_ref[...])
        x = jnp.concatenate([lo, hi], -1).astype(BF16)
        gate = jnp.minimum(_dot(x, wgb_ref[...]) + bg_ref[0], SWIGLU_LIMIT)
        up = jnp.clip(_dot(x, wub_ref[...]) + bu_ref[0], -SWIGLU_LIMIT, SWIGLU_LIMIT)
        act = (up + 1.0) * (gate * jax.nn.sigmoid(SWIGLU_ALPHA * gate))
        y_ref[...] = _pack_bf16_pair(_dot(act.astype(BF16), wdb_ref[...]) + bd_ref[0])

    @pl.when(jnp.logical_not(valid))
    def _():
        y_ref[...] = jnp.zeros_like(y_ref)


def _experts(blk_e, n_valid, xs, wg, bg, wu, bu, wd, bd, tb):
    n_rows = xs.shape[0]
    wspec = pl.BlockSpec((1, D_MODEL, D_FF), lambda j, be, nv: (be[j], 0, 0))
    bspec = pl.BlockSpec((1, 1, D_FF), lambda j, be, nv: (be[j], 0, 0))
    return pl.pallas_call(
        _expert_kernel,
        grid_spec=pltpu.PrefetchScalarGridSpec(
            num_scalar_prefetch=2,
            grid=(n_rows // tb,),
            in_specs=[pl.BlockSpec((tb, HALF), lambda j, be, nv: (jnp.minimum(j, nv[0] - 1), 0)),
                      wspec, bspec, wspec, bspec, wspec, bspec],
            out_specs=pl.BlockSpec((tb, HALF), lambda j, be, nv: (j, 0)),
            scratch_shapes=[pltpu.VMEM((D_MODEL, D_FF), BF16), pltpu.VMEM((D_MODEL, D_FF), BF16),
                            pltpu.VMEM((D_FF, D_MODEL), BF16)],
        ),
        out_shape=jax.ShapeDtypeStruct((n_rows, HALF), I32),
        compiler_params=_params("arbitrary"),
        name="moe_experts",
    )(blk_e, n_valid, xs, wg, bg, wu, bu, wd, bd)


def _sc_gather_kernel(table_hbm, idx_hbm, out_hbm, idx_v, rows_v, sem, *, rows_per_worker):
    base = _sc_worker() * rows_per_worker

    @pl.loop(0, rows_per_worker // SC_GATHER_ROWS)
    def _(j):
        off = base + j * SC_GATHER_ROWS
        pltpu.sync_copy(idx_hbm.at[pl.ds(off, SC_GATHER_ROWS)], idx_v)
        pltpu.async_copy(table_hbm.at[idx_v], rows_v, sem).wait()
        pltpu.sync_copy(rows_v, out_hbm.at[pl.ds(off, SC_GATHER_ROWS)])


def _sc_gather_rows(table, idx):
    n, width = idx.shape[0], table.shape[1]
    workers = V7X_SC_CORES * V7X_SC_SUBCORES
    assert n % (workers * SC_GATHER_ROWS) == 0
    return pl.kernel(
        functools.partial(_sc_gather_kernel, rows_per_worker=n // workers),
        out_type=jax.ShapeDtypeStruct((n, width), table.dtype),
        mesh=_sc_mesh(),
        scratch_types=[pltpu.VMEM((SC_GATHER_ROWS,), I32), pltpu.VMEM((SC_GATHER_ROWS, width), table.dtype),
                       pltpu.SemaphoreType.DMA],
        name="moe_gather_sc",
    )(table, idx)


def _ln2_kernel(y4_ref, h_ref, tw_ref, g2_ref, b2_ref, o_ref):
    tw = tw_ref[...]
    f_lo = jnp.zeros(y4_ref.shape[1:], F32)
    f_hi = jnp.zeros(y4_ref.shape[1:], F32)
    for k in range(TOP_K):
        lo, hi = _unpack_bf16_pair(y4_ref[k])
        w = tw[:, k:k + 1]
        f_lo = f_lo + w * lo
        f_hi = f_hi + w * hi
    f = jnp.concatenate([f_lo, f_hi], -1)
    o_ref[...] = _layer_norm(DEEPNORM_ALPHA * h_ref[...] + f, g2_ref[...], b2_ref[...])


def _weighted_sum_ln2(y4, h, top_w, ln_g, ln_b, tm):
    T = h.shape[0]
    tm = min(tm, T)
    full = lambda a: pl.BlockSpec(a.shape, lambda i: (0,) * a.ndim)
    return pl.pallas_call(
        _ln2_kernel,
        grid=(T // tm,),
        in_specs=[pl.BlockSpec((TOP_K, tm, HALF), lambda i: (0, i, 0)),
                  pl.BlockSpec((tm, D_MODEL), lambda i: (i, 0)),
                  pl.BlockSpec((tm, TOP_K), lambda i: (i, 0)),
                  full(ln_g), full(ln_b)],
        out_specs=pl.BlockSpec((tm, D_MODEL), lambda i: (i, 0)),
        out_shape=jax.ShapeDtypeStruct((T, D_MODEL), F32),
        compiler_params=_params("parallel"),
        name="moe_sum_ln2",
    )(y4, h, top_w, ln_g, ln_b)


def _rope_table(seq, dim):
    inv = ROPE_THETA ** (-jnp.arange(0, dim, 2, dtype=F32) / dim)
    ang = jnp.arange(seq, dtype=F32)[:, None] * inv[None, :]
    return jnp.cos(ang), jnp.sin(ang)


def _prepare_weights(w_in, ret_decay_logit, ret_gn_g, w_ret_o, mla_q_norm_g, w_uq, mla_kv_norm_g,
                     w_uk, w_uv, w_mla_o, w_out, ln1_g, ln1_b, router_w, router_b, exp_w_gate,
                     exp_b_gate, exp_w_up, exp_b_up, exp_w_down, exp_b_down, ln2_g, ln2_b):
    o = 0
    seg = {}
    for name, width in (("q", RET_QK_W), ("k", RET_QK_W), ("v", RET_V_W), ("g", RET_V_W),
                        ("cq", Q_LORA), ("ckv", KV_LORA), ("kpe", MLA_ROPE),
                        ("ga", D_MODEL), ("gb", D_MODEL)):
        seg[name] = w_in[:, o:o + width]
        o += width
    half = MLA_ROPE // 2
    swap = lambda w: jnp.concatenate([w[:, half:], w[:, :half]], -1)
    w_qk = jnp.concatenate([seg["q"], seg["k"]], -1).astype(BF16)
    w_vg = jnp.concatenate([seg["v"], seg["g"], seg["ga"], seg["gb"]], -1).astype(BF16)
    w_mla = jnp.concatenate([seg["cq"], seg["ckv"], seg["kpe"], swap(seg["kpe"])], -1).astype(BF16)
    wq = w_uq.reshape(Q_LORA, MLA_HEADS, MLA_NOPE + MLA_ROPE)
    wq_pe = wq[:, :, MLA_NOPE:]
    wq_ext = jnp.concatenate([wq[:, :, :MLA_NOPE], wq_pe,
                              jnp.concatenate([wq_pe[:, :, half:], wq_pe[:, :, :half]], -1)], -1)
    rw_hi = router_w.astype(BF16)
    return dict(
        w_qk=w_qk, w_vg=w_vg, w_mla=w_mla,
        log_gamma=jax.nn.log_sigmoid(ret_decay_logit.astype(F32)),
        gn_g=ret_gn_g.reshape(1, RET_V_W),
        w_ret_o=w_ret_o.astype(BF16),
        qn_g=mla_q_norm_g.reshape(1, Q_LORA), kvn_g=mla_kv_norm_g.reshape(1, KV_LORA),
        wq_ext=wq_ext.reshape(Q_LORA, MLA_HEADS * MLA_QK).astype(BF16),
        wk2=w_uk.reshape(KV_LORA, MLA_HEADS * MLA_NOPE).astype(BF16),
        wv2=w_uv.reshape(KV_LORA, MLA_HEADS * MLA_V).astype(BF16),
        w_mla_o=w_mla_o.astype(BF16), w_out=w_out.astype(BF16),
        ln1_g=ln1_g.reshape(1, D_MODEL), ln1_b=ln1_b.reshape(1, D_MODEL),
        rw_hi=rw_hi,
        rw_cat=jnp.concatenate([rw_hi, (router_w - rw_hi.astype(F32)).astype(BF16)], -1),
        rb=router_b.reshape(1, N_EXPERTS),
        wg=exp_w_gate, bg=exp_b_gate.reshape(N_EXPERTS, 1, D_FF),
        wu=exp_w_up, bu=exp_b_up.reshape(N_EXPERTS, 1, D_FF),
        wd=exp_w_down, bd=exp_b_down.reshape(N_EXPERTS, 1, D_MODEL),
        ln2_g=ln2_g.reshape(1, D_MODEL), ln2_b=ln2_b.reshape(1, D_MODEL),
    )


def _encoder_layer(x3, p):
    B, S, D = x3.shape
    T = B * S
    x = x3.reshape(T, D)

    cos_r, sin_r = _rope_table(S, RET_DK)
    cos_m, sin_m = _rope_table(S, MLA_ROPE)
    zeros = jnp.zeros((S, 128 - MLA_ROPE), F32)
    cc = jnp.concatenate([cos_m, cos_m, zeros], -1)
    ss = jnp.concatenate([-sin_m, sin_m, zeros], -1)

    qk = _qk_rope_proj(x, p["w_qk"], cos_r, sin_r, S, tm=2048)
    vg = _linear(x, p["w_vg"], BF16, tm=2048, tn=1024, name="proj_vg")
    mla = _linear(x, p["w_mla"], F32, tm=1024, tn=MLA_SEG, name="proj_mla")

    y_ret = _retention(p["log_gamma"], qk, vg, p["gn_g"], B, S, chunk=256).reshape(T, RET_V_W)
    q, k, v = _mla_prep(mla, cc, ss, p["qn_g"], p["kvn_g"], p["wq_ext"], p["wk2"], p["wv2"], B, S, tm=512)
    o_mla = _attention(q, k, v, ATTN_SCORE_BYTES).reshape(T, MLA_HEADS * MLA_V)

    h, hp, top_e, top_w = _merge(x, y_ret, o_mla, vg, p["w_ret_o"], p["w_mla_o"], p["w_out"],
                                 p["ln1_g"], p["ln1_b"], p["rw_hi"], p["rw_cat"], p["rb"], tm=512)

    rank, counts = _rank(top_e, tm=512)
    tb = EXPERT_BLOCK
    n_rows = T * TOP_K + N_EXPERTS * tb
    counts = counts.reshape(N_EXPERTS)
    padded = (counts + tb - 1) // tb * tb
    pend = jnp.cumsum(padded)
    pstart = pend - padded
    dest = pstart[top_e] + rank
    blk_start = jnp.arange(n_rows // tb, dtype=I32) * tb
    blk_e = jnp.minimum(jnp.sum((pend[None, :] <= blk_start[:, None]).astype(I32), -1), N_EXPERTS - 1)
    n_valid = (pend[-1:] // tb).astype(I32)

    seg = jnp.concatenate([padded - counts, n_rows - pend[-1:]])
    seg_end = jnp.cumsum(seg)
    seg_base = jnp.concatenate([pstart + counts, pend[-1:]])
    j = jnp.arange(n_rows - T * TOP_K, dtype=I32)
    s = jnp.sum((seg_end[None, :] <= j[:, None]).astype(I32), -1)
    pad_rows = (seg_base[s] + j - (seg_end - seg)[s]).astype(I32)

    R = SC_GATHER_ROWS
    dest_chunks = dest.reshape(T // R, R, TOP_K).transpose(0, 2, 1)
    xs = _sc_dispatch(hp, dest_chunks, pad_rows.reshape(-1, R), n_rows)
    ys = _experts(blk_e, n_valid, xs, p["wg"], p["bg"], p["wu"], p["bu"], p["wd"], p["bd"], tb)
    y4 = _sc_gather_rows(ys, dest.T.reshape(T * TOP_K)).reshape(TOP_K, T, HALF)
    out = _weighted_sum_ln2(y4, h, top_w, p["ln2_g"], p["ln2_b"], tm=512)
    return out.reshape(B, S, D)


def kernel(x_prompt, x_sample, w_in, ret_decay_logit, ret_gn_g, w_ret_o, mla_q_norm_g, w_uq, mla_kv_norm_g, w_uk, w_uv, w_mla_o, w_out, ln1_g, ln1_b, router_w, router_b, exp_w_gate, exp_b_gate, exp_w_up, exp_b_up, exp_w_down, exp_b_down, ln2_g, ln2_b):
    params = (w_in, ret_decay_logit, ret_gn_g, w_ret_o, mla_q_norm_g, w_uq, mla_kv_norm_g, w_uk, w_uv,
              w_mla_o, w_out, ln1_g, ln1_b, router_w, router_b, exp_w_gate, exp_b_gate, exp_w_up,
              exp_b_up, exp_w_down, exp_b_down, ln2_g, ln2_b)
    depth = w_in.shape[0]
    y_prompt, y_sample = x_prompt, x_sample
    for l in range(depth):
        p = _prepare_weights(*[w[l] for w in params])
        y_prompt = _encoder_layer(y_prompt, p)
        y_sample = _encoder_layer(y_sample, p)
    return (y_prompt, y_sample)
```

```python
import functools

import jax
import jax.numpy as jnp
from jax import lax
from jax.experimental import pallas as pl
from jax.experimental.pallas import tpu as pltpu
from jax.experimental.pallas import tpu_sc as plsc

F32 = jnp.float32
BF16 = jnp.bfloat16
U32 = jnp.uint32
I32 = jnp.int32

D_MODEL = 1024
RET_HEADS = 4
RET_DK = 256
RET_DV = 512
RET_CHUNK = 256
MLA_HEADS = 8
MLA_NOPE = 128
MLA_ROPE = 64
MLA_V = 128
Q_LORA = 384
KV_LORA = 256
ROPE_THETA = 10000.0
N_EXPERTS = 32
TOP_K = 4
D_FF = 1024
SWIGLU_LIMIT = 7.0
SWIGLU_ALPHA = 1.702
LN_EPS = 1e-5
RMS_EPS = 1e-6
DEEPNORM_ALPHA = 2.0 ** 0.25
RET_QK_W = RET_HEADS * RET_DK
RET_V_W = RET_HEADS * RET_DV
MLA_SEG = Q_LORA + KV_LORA + 2 * MLA_ROPE
MLA_QK = 256
LOG2E = 1.4426950408889634

V7X_VMEM_LIMIT = 56 * 1024 * 1024
V7X_SC_CORES = 2
V7X_SC_SUBCORES = 16
SC_GATHER_ROWS = 128
EXPERT_BLOCK = 512
ATTN_ROW_GROUP = 256
ATTN_SCORE_BYTES = 32 * 1024 * 1024
HALF = D_MODEL // 2


def _params(*sem):
    return pltpu.CompilerParams(dimension_semantics=sem, vmem_limit_bytes=V7X_VMEM_LIMIT)


def _nt_dot(a, b):
    return lax.dot_general(a, b, (((1,), (1,)), ((), ())), preferred_element_type=F32)


def _tn_dot(a, b):
    return lax.dot_general(a, b, (((0,), (0,)), ((), ())), preferred_element_type=F32)


def _dot(a, b):
    return jnp.dot(a, b, preferred_element_type=F32)


def _pack_bf16_pair(x):
    bits = lax.bitcast_convert_type(x.astype(BF16).astype(F32), U32)
    return lax.bitcast_convert_type((bits[:, :HALF] >> 16) | bits[:, HALF:], I32)


def _unpack_bf16_pair(p):
    p = lax.bitcast_convert_type(p, U32)
    lo = lax.bitcast_convert_type(p << 16, F32)
    hi = lax.bitcast_convert_type(p & jnp.uint32(0xFFFF0000), F32)
    return lo, hi


def _linear_kernel(x_ref, w_ref, o_ref, xb_ref):
    @pl.when(pl.program_id(1) == 0)
    def _():
        xb_ref[...] = x_ref[...].astype(BF16)

    o_ref[...] = _dot(xb_ref[...], w_ref[...]).astype(o_ref.dtype)


def _linear(x, w, out_dtype, tm, tn, name):
    M, K = x.shape
    N = w.shape[1]
    tm, tn = min(tm, M), min(tn, N)
    return pl.pallas_call(
        _linear_kernel,
        grid=(M // tm, N // tn),
        in_specs=[pl.BlockSpec((tm, K), lambda i, j: (i, 0)),
                  pl.BlockSpec((K, tn), lambda i, j: (0, j))],
        out_specs=pl.BlockSpec((tm, tn), lambda i, j: (i, j)),
        out_shape=jax.ShapeDtypeStruct((M, N), out_dtype),
        scratch_shapes=[pltpu.VMEM((tm, K), BF16)],
        compiler_params=_params("parallel", "arbitrary"),
        name=name,
    )(x, w)


def _qk_rope_kernel(x_ref, w_ref, cos_ref, sin_ref, o_ref, xb_ref, *, q_scale):
    j = pl.program_id(1)

    @pl.when(j == 0)
    def _():
        xb_ref[...] = x_ref[...].astype(BF16)

    y = _dot(xb_ref[...], w_ref[...])
    half = RET_DK // 2
    heads = y.shape[1] // RET_DK
    c, s = cos_ref[...], sin_ref[...]
    scale = jnp.where(j < RET_HEADS // heads, q_scale, 1.0).astype(F32)
    for h in range(heads):
        x1 = y[:, h * RET_DK:h * RET_DK + half]
        x2 = y[:, h * RET_DK + half:(h + 1) * RET_DK]
        o_ref[:, h * RET_DK:h * RET_DK + half] = ((x1 * c - x2 * s) * scale).astype(o_ref.dtype)
        o_ref[:, h * RET_DK + half:(h + 1) * RET_DK] = ((x1 * s + x2 * c) * scale).astype(o_ref.dtype)


def _qk_rope_proj(x, w_qk, cos, sin, seq, tm):
    M, K = x.shape
    N = w_qk.shape[1]
    tm = min(tm, seq)
    nseq = seq // tm
    tn = RET_HEADS * RET_DK
    return pl.pallas_call(
        functools.partial(_qk_rope_kernel, q_scale=RET_DK ** -0.5),
        grid=(M // tm, N // tn),
        in_specs=[pl.BlockSpec((tm, K), lambda i, j: (i, 0)),
                  pl.BlockSpec((K, tn), lambda i, j: (0, j)),
                  pl.BlockSpec((tm, RET_DK // 2), lambda i, j: (i % nseq, 0)),
                  pl.BlockSpec((tm, RET_DK // 2), lambda i, j: (i % nseq, 0))],
        out_specs=pl.BlockSpec((tm, tn), lambda i, j: (i, j)),
        out_shape=jax.ShapeDtypeStruct((M, N), BF16),
        scratch_shapes=[pltpu.VMEM((tm, K), BF16)],
        compiler_params=_params("parallel", "arbitrary"),
        name="qk_rope_proj",
    )(x, w_qk, cos, sin)


def _retention_kernel(lg_ref, q_ref, k_ref, v_ref, g_ref, gn_ref, o_ref, rf_ref, rb_ref, sf_ref, sb_ref,
                      *, chunk):
    h = pl.program_id(1)
    S = q_ref.shape[1]
    C = chunk
    NC = S // C
    lgf = lg_ref[0, h]
    lgb = lg_ref[1, h]

    n_col = lax.broadcasted_iota(I32, (C, 1), 0).astype(F32)
    diff = (lax.broadcasted_iota(I32, (C, C), 0) - lax.broadcasted_iota(I32, (C, C), 1)).astype(F32)
    decay = jnp.where(diff >= 0, jnp.exp(lgf * jnp.maximum(diff, 0.0)),
                      jnp.exp(lgb * jnp.maximum(-diff, 0.0)))
    xi_f = jnp.exp(lgf * (n_col + 1.0))
    zeta_f = jnp.exp(lgf * (C - 1.0 - n_col))
    xi_b = jnp.exp(lgb * (C - n_col))
    zeta_b = jnp.exp(lgb * n_col)
    ones = jnp.ones((1, 1), F32)
    g_f = jnp.exp(ones * (lgf * C))
    g_b = jnp.exp(ones * (lgb * C))

    def rows(i):
        return pl.ds(pl.multiple_of(i * C, C), C)

    sf_ref[...] = jnp.zeros_like(sf_ref)
    sb_ref[...] = jnp.zeros_like(sb_ref)

    def scan(j, carry):
        i_f, i_b = j, NC - 1 - j
        rf_ref[i_f] = sf_ref[...].astype(BF16)
        rb_ref[i_b] = sb_ref[...].astype(BF16)
        kf = (k_ref[0, rows(i_f), :].astype(F32) * zeta_f).astype(BF16)
        kb = (k_ref[0, rows(i_b), :].astype(F32) * zeta_b).astype(BF16)
        sf_ref[...] = sf_ref[...] * g_f + _tn_dot(kf, v_ref[0, rows(i_f), :])
        sb_ref[...] = sb_ref[...] * g_b + _tn_dot(kb, v_ref[0, rows(i_b), :])
        return carry

    lax.fori_loop(0, NC, scan, 0, unroll=4 if NC % 4 == 0 else 1)

    gn = gn_ref[...]

    def emit(i, carry):
        r = rows(i)
        qb = q_ref[0, r, :]
        q = qb.astype(F32)
        v = v_ref[0, r, :]
        inner = _nt_dot(qb, k_ref[0, r, :]) * decay
        o = (_dot(inner.astype(BF16), v)
             + _dot((q * xi_f).astype(BF16), rf_ref[i])
             + _dot((q * xi_b).astype(BF16), rb_ref[i]))
        mu = jnp.mean(o, -1, keepdims=True)
        d = o - mu
        var = jnp.mean(d * d, -1, keepdims=True)
        on = d * lax.rsqrt(var + LN_EPS) * gn
        gate = g_ref[0, r, :].astype(F32)
        o_ref[0, r, :] = (gate * jax.nn.sigmoid(gate) * on).astype(o_ref.dtype)
        return carry

    lax.fori_loop(0, NC, emit, 0, unroll=8 if NC % 8 == 0 else 1)


def _retention(log_gamma, qk, vg, gn_g, B, S, chunk):
    qk3 = qk.reshape(B, S, 2 * RET_QK_W)
    vg3 = vg.reshape(B, S, vg.shape[1])
    H = RET_HEADS
    chunk = min(chunk, S)
    nc = S // chunk
    return pl.pallas_call(
        functools.partial(_retention_kernel, chunk=chunk),
        grid=(B, H),
        in_specs=[pl.BlockSpec(memory_space=pltpu.SMEM),
                  pl.BlockSpec((1, S, RET_DK), lambda b, h: (b, 0, h)),
                  pl.BlockSpec((1, S, RET_DK), lambda b, h: (b, 0, H + h)),
                  pl.BlockSpec((1, S, RET_DV), lambda b, h: (b, 0, h)),
                  pl.BlockSpec((1, S, RET_DV), lambda b, h: (b, 0, H + h)),
                  pl.BlockSpec((1, RET_DV), lambda b, h: (0, h))],
        out_specs=pl.BlockSpec((1, S, RET_DV), lambda b, h: (b, 0, h)),
        out_shape=jax.ShapeDtypeStruct((B, S, RET_V_W), BF16),
        scratch_shapes=[pltpu.VMEM((nc, RET_DK, RET_DV), BF16), pltpu.VMEM((nc, RET_DK, RET_DV), BF16),
                        pltpu.VMEM((RET_DK, RET_DV), F32), pltpu.VMEM((RET_DK, RET_DV), F32)],
        compiler_params=_params("parallel", "arbitrary"),
        name="retention",
    )(log_gamma, qk3, qk3, vg3, vg3, gn_g)


def _rms(x, g):
    return x * lax.rsqrt(jnp.mean(x * x, -1, keepdims=True) + RMS_EPS) * g


def _mla_prep_kernel(m_ref, cc_ref, ss_ref, qg_ref, kg_ref, wq_ref, wk_ref, wv_ref,
                     q_ref, k_ref, v_ref, *, q_scale):
    x = m_ref[...]
    cc, ss = cc_ref[...], ss_ref[...]

    def rope(pair):
        return pair * cc + pltpu.roll(pair, MLA_ROPE, 1) * ss

    c_q = _rms(x[:, :Q_LORA], qg_ref[...]).astype(BF16)
    c_kv = _rms(x[:, Q_LORA:Q_LORA + KV_LORA], kg_ref[...]).astype(BF16)
    k_pe = rope(x[:, Q_LORA + KV_LORA:]).astype(BF16)
    q_all = _dot(c_q, wq_ref[...])
    k_all = _dot(c_kv, wk_ref[...])
    v_all = _dot(c_kv, wv_ref[...])
    lane = lax.broadcasted_iota(I32, (x.shape[0], MLA_V), 1)
    ones_col = jnp.where(lane == 0, 1.0, 0.0).astype(BF16)
    for h in range(MLA_HEADS):
        qh = q_all[:, h * MLA_QK:(h + 1) * MLA_QK]
        q_ref[0, h, :, :MLA_NOPE] = (qh[:, :MLA_NOPE] * q_scale).astype(BF16)
        q_ref[0, h, :, MLA_NOPE:] = (rope(qh[:, MLA_NOPE:]) * q_scale).astype(BF16)
        k_ref[0, h, :, :MLA_NOPE] = k_all[:, h * MLA_NOPE:(h + 1) * MLA_NOPE].astype(BF16)
        k_ref[0, h, :, MLA_NOPE:] = k_pe
        v_ref[0, h, :, :MLA_V] = v_all[:, h * MLA_V:(h + 1) * MLA_V].astype(BF16)
        v_ref[0, h, :, MLA_V:] = ones_col


def _mla_prep(mla, cc, ss, qn_g, kvn_g, wq_ext, wk2, wv2, B, S, tm):
    tm = min(tm, S)
    ns = S // tm
    H = MLA_HEADS
    q_scale = (MLA_NOPE + MLA_ROPE) ** -0.5 * LOG2E
    full = lambda shape: pl.BlockSpec(shape, lambda b, i: (0,) * len(shape))
    return pl.pallas_call(
        functools.partial(_mla_prep_kernel, q_scale=q_scale),
        grid=(B, ns),
        in_specs=[pl.BlockSpec((tm, MLA_SEG), lambda b, i: (b * ns + i, 0)),
                  pl.BlockSpec((tm, 128), lambda b, i: (i, 0)),
                  pl.BlockSpec((tm, 128), lambda b, i: (i, 0)),
                  full((1, Q_LORA)), full((1, KV_LORA)),
                  full(wq_ext.shape), full(wk2.shape), full(wv2.shape)],
        out_specs=[pl.BlockSpec((1, H, tm, MLA_QK), lambda b, i: (b, 0, i, 0)),
                   pl.BlockSpec((1, H, tm, MLA_QK), lambda b, i: (b, 0, i, 0)),
                   pl.BlockSpec((1, H, tm, 2 * MLA_V), lambda b, i: (b, 0, i, 0))],
        out_shape=[jax.ShapeDtypeStruct((B, H, S, MLA_QK), BF16),
                   jax.ShapeDtypeStruct((B, H, S, MLA_QK), BF16),
                   jax.ShapeDtypeStruct((B, H, S, 2 * MLA_V), BF16)],
        compiler_params=_params("parallel", "parallel"),
        name="mla_prep",
    )(mla, cc, ss, qn_g, kvn_g, wq_ext, wk2, wv2)


def _attention_kernel(q_ref, k_ref, v_ref, o_ref, *, n_sub):
    tq = q_ref.shape[2]
    ts = tq // n_sub
    k = k_ref[0, 0]
    v = v_ref[0, 0]
    for u in range(n_sub):
        rows = slice(u * ts, (u + 1) * ts)
        s = _nt_dot(q_ref[0, 0, rows, :], k)
        m = jnp.max(s, -1, keepdims=True)
        p = jnp.exp2(s - m)
        o = _dot(p.astype(BF16), v)
        o_ref[0, rows, :] = (o[:, :MLA_V] / o[:, MLA_V:MLA_V + 1]).astype(o_ref.dtype)


def _attention(q, k, v, score_bytes):
    B, H, S, _ = q.shape
    ts = min(ATTN_ROW_GROUP, S)
    n_sub = max(min(score_bytes // (ts * S * 4), S // ts), 1)
    tq = ts * n_sub
    return pl.pallas_call(
        functools.partial(_attention_kernel, n_sub=n_sub),
        grid=(B, H, S // tq),
        in_specs=[pl.BlockSpec((1, 1, tq, MLA_QK), lambda b, h, i: (b, h, i, 0)),
                  pl.BlockSpec((1, 1, S, MLA_QK), lambda b, h, i: (b, h, 0, 0)),
                  pl.BlockSpec((1, 1, S, 2 * MLA_V), lambda b, h, i: (b, h, 0, 0))],
        out_specs=pl.BlockSpec((1, tq, MLA_V), lambda b, h, i: (b, i, h)),
        out_shape=jax.ShapeDtypeStruct((B, S, H * MLA_V), BF16),
        compiler_params=_params("parallel", "parallel", "arbitrary"),
        name="attention",
    )(q, k, v)


def _layer_norm(x, g, b):
    mu = jnp.mean(x, -1, keepdims=True)
    d = x - mu
    var = jnp.mean(d * d, -1, keepdims=True)
    return d * lax.rsqrt(var + LN_EPS) * g + b


def _merge_kernel(x_ref, yr_ref, om_ref, gate_ref, wro_ref, wmo_ref, wout_ref, g1_ref, b1_ref,
                  rwh_ref, rwc_ref, rb_ref, h_ref, hp_ref, te_ref, tw_ref, *, n_sub):
    ts = x_ref.shape[0] // n_sub
    for u in range(n_sub):
        r = slice(u * ts, (u + 1) * ts)
        y_a = _dot(yr_ref[r, :], wro_ref[...])
        y_b = _dot(om_ref[r, :], wmo_ref[...])
        gates = gate_ref[r, :].astype(F32)
        merged = (jax.nn.sigmoid(gates[:, :D_MODEL]) * y_a + jax.nn.sigmoid(gates[:, D_MODEL:]) * y_b)
        mix = _dot(merged.astype(BF16), wout_ref[...])
        h = _layer_norm(DEEPNORM_ALPHA * x_ref[r, :] + mix, g1_ref[...], b1_ref[...])
        h_ref[r, :] = h
        hp_ref[r, :] = _pack_bf16_pair(h)

        h_hi = h.astype(BF16)
        h_lo = (h - h_hi.astype(F32)).astype(BF16)
        both = _dot(h_hi, rwc_ref[...])
        logits = both[:, :N_EXPERTS] + both[:, N_EXPERTS:] + _dot(h_lo, rwh_ref[...]) + rb_ref[...]
        lane = lax.broadcasted_iota(I32, logits.shape, 1)
        vals, idxs = [], []
        for _ in range(TOP_K):
            m = jnp.max(logits, -1, keepdims=True)
            idx = jnp.min(jnp.where(logits == m, lane, N_EXPERTS), -1, keepdims=True)
            vals.append(m)
            idxs.append(idx)
            logits = jnp.where(lane == idx, -jnp.inf, logits)
        e = [jnp.exp(v - vals[0]) for v in vals]
        tot = e[0] + e[1] + e[2] + e[3]
        te_ref[r, :] = jnp.concatenate(idxs, -1)
        tw_ref[r, :] = jnp.concatenate([ek / tot for ek in e], -1)


def _merge(x, y_ret, o_mla, vg, w_ret_o, w_mla_o, w_out, ln_g, ln_b, rw_hi, rw_cat, rb, tm):
    T = x.shape[0]
    tm = min(tm, T)
    gate_blk = (2 * RET_V_W) // (2 * D_MODEL)
    full = lambda a: pl.BlockSpec(a.shape, lambda i: (0,) * a.ndim)
    row = lambda w: pl.BlockSpec((tm, w), lambda i: (i, 0))
    return pl.pallas_call(
        functools.partial(_merge_kernel, n_sub=1),
        grid=(T // tm,),
        in_specs=[row(D_MODEL), row(RET_V_W), row(D_MODEL),
                  pl.BlockSpec((tm, 2 * D_MODEL), lambda i: (i, gate_blk)),
                  full(w_ret_o), full(w_mla_o), full(w_out), full(ln_g), full(ln_b),
                  full(rw_hi), full(rw_cat), full(rb)],
        out_specs=[row(D_MODEL), row(HALF), row(TOP_K), row(TOP_K)],
        out_shape=[jax.ShapeDtypeStruct((T, D_MODEL), F32),
                   jax.ShapeDtypeStruct((T, HALF), I32),
                   jax.ShapeDtypeStruct((T, TOP_K), I32),
                   jax.ShapeDtypeStruct((T, TOP_K), F32)],
        compiler_params=_params("parallel"),
        name="merge_ln1_router",
    )(x, y_ret, o_mla, vg, w_ret_o, w_mla_o, w_out, ln_g, ln_b, rw_hi, rw_cat, rb)


def _rank_kernel(te_ref, rank_ref, cnt_ref, acc_ref):
    @pl.when(pl.program_id(0) == 0)
    def _():
        acc_ref[...] = jnp.zeros_like(acc_ref)

    tm = te_ref.shape[0]
    te = te_ref[...]
    lane = lax.broadcasted_iota(I32, (tm, N_EXPERTS), 1)
    earlier = (lax.broadcasted_iota(I32, (tm, tm), 0) > lax.broadcasted_iota(I32, (tm, tm), 1))
    earlier = jnp.where(earlier, 1.0, 0.0).astype(BF16)
    base = acc_ref[...]
    ranks = []
    for k in range(TOP_K):
        onehot = jnp.where(lane == te[:, k:k + 1], 1.0, 0.0)
        before = _dot(earlier, onehot.astype(BF16))
        ranks.append(jnp.sum(onehot * (before + base), -1, keepdims=True))
        base = base + jnp.sum(onehot, 0, keepdims=True)
    acc_ref[...] = base
    rank_ref[...] = jnp.concatenate(ranks, -1).astype(I32)
    cnt_ref[...] = base.astype(I32)


def _rank(top_e, tm):
    T = top_e.shape[0]
    tm = min(tm, T)
    return pl.pallas_call(
        _rank_kernel,
        grid=(T // tm,),
        in_specs=[pl.BlockSpec((tm, TOP_K), lambda i: (i, 0))],
        out_specs=[pl.BlockSpec((tm, TOP_K), lambda i: (i, 0)),
                   pl.BlockSpec((1, N_EXPERTS), lambda i: (0, 0))],
        out_shape=[jax.ShapeDtypeStruct((T, TOP_K), I32),
                   jax.ShapeDtypeStruct((1, N_EXPERTS), I32)],
        scratch_shapes=[pltpu.VMEM((1, N_EXPERTS), F32)],
        compiler_params=_params("arbitrary"),
        name="moe_rank",
    )(top_e)


def _sc_mesh():
    return plsc.VectorSubcoreMesh(core_axis_name="c", subcore_axis_name="s",
                                  num_cores=V7X_SC_CORES, num_subcores=V7X_SC_SUBCORES)


def _sc_worker():
    return lax.axis_index("s") * V7X_SC_CORES + lax.axis_index("c")


def _sc_dispatch_kernel(hp_hbm, dest_hbm, pad_hbm, zeros_hbm, xs_hbm, idx_v, rows_v, sem,
                        *, chunks_per_worker, pad_chunks_per_worker):
    R = SC_GATHER_ROWS
    worker = _sc_worker()

    @pl.loop(0, chunks_per_worker)
    def _(j):
        chunk = worker * chunks_per_worker + j
        pltpu.sync_copy(hp_hbm.at[pl.ds(chunk * R, R)], rows_v)
        pltpu.sync_copy(dest_hbm.at[chunk], idx_v)
        copies = [pltpu.async_copy(rows_v, xs_hbm.at[idx_v.at[k]], sem) for k in range(TOP_K)]
        for cp in copies:
            cp.wait()

    pltpu.sync_copy(zeros_hbm, rows_v)

    @pl.loop(0, pad_chunks_per_worker)
    def _(j):
        chunk = worker * pad_chunks_per_worker + j
        pltpu.sync_copy(pad_hbm.at[chunk], idx_v.at[0])
        pltpu.async_copy(rows_v, xs_hbm.at[idx_v.at[0]], sem).wait()


def _sc_dispatch(hp, dest_chunks, pad_chunks, n_rows):
    R = SC_GATHER_ROWS
    workers = V7X_SC_CORES * V7X_SC_SUBCORES
    n_chunks, n_pad_chunks = dest_chunks.shape[0], pad_chunks.shape[0]
    assert n_chunks % workers == 0 and n_pad_chunks % workers == 0
    return pl.kernel(
        functools.partial(_sc_dispatch_kernel, chunks_per_worker=n_chunks // workers,
                          pad_chunks_per_worker=n_pad_chunks // workers),
        out_type=jax.ShapeDtypeStruct((n_rows, HALF), I32),
        mesh=_sc_mesh(),
        scratch_types=[pltpu.VMEM((TOP_K, R), I32), pltpu.VMEM((R, HALF), I32), pltpu.SemaphoreType.DMA],
        name="moe_dispatch_sc",
    )(hp, dest_chunks, pad_chunks, jnp.zeros((R, HALF), I32))


def _expert_kernel(be_ref, nv_ref, x_ref, wg_ref, bg_ref, wu_ref, bu_ref, wd_ref, bd_ref, y_ref,
                   wgb_ref, wub_ref, wdb_ref):
    j = pl.program_id(0)
    valid = j < nv_ref[0]
    new_expert = jnp.logical_or(j == 0, be_ref[j] != be_ref[jnp.maximum(j - 1, 0)])

    @pl.when(jnp.logical_and(valid, new_expert))
    def _():
        wgb_ref[...] = wg_ref[0].astype(BF16)
        wub_ref[...] = wu_ref[0].astype(BF16)
        wdb_ref[...] = wd_ref[0].astype(BF16)

    @pl.when(valid)
    def _():
        lo, hi = _unpack_bf16_pair(x_ref[...])
        x = jnp.concatenate([lo, hi], -1).astype(BF16)
        gate = jnp.minimum(_dot(x, wgb_ref[...]) + bg_ref[0], SWIGLU_LIMIT)
        up = jnp.clip(_dot(x, wub_ref[...]) + bu_ref[0], -SWIGLU_LIMIT, SWIGLU_LIMIT)
        act = (up + 1.0) * (gate * jax.nn.sigmoid(SWIGLU_ALPHA * gate))
        y_ref[...] = _pack_bf16_pair(_dot(act.astype(BF16), wdb_ref[...]) + bd_ref[0])

    @pl.when(jnp.logical_not(valid))
    def _():
        y_ref[...] = jnp.zeros_like(y_ref)


def _experts(blk_e, n_valid, xs, wg, bg, wu, bu, wd, bd, tb):
    n_rows = xs.shape[0]
    wspec = pl.BlockSpec((1, D_MODEL, D_FF), lambda j, be, nv: (be[j], 0, 0))
    bspec = pl.BlockSpec((1, 1, D_FF), lambda j, be, nv: (be[j], 0, 0))
    return pl.pallas_call(
        _expert_kernel,
        grid_spec=pltpu.PrefetchScalarGridSpec(
            num_scalar_prefetch=2,
            grid=(n_rows // tb,),
            in_specs=[pl.BlockSpec((tb, HALF), lambda j, be, nv: (jnp.minimum(j, nv[0] - 1), 0)),
                      wspec, bspec, wspec, bspec, wspec, bspec],
            out_specs=pl.BlockSpec((tb, HALF), lambda j, be, nv: (j, 0)),
            scratch_shapes=[pltpu.VMEM((D_MODEL, D_FF), BF16), pltpu.VMEM((D_MODEL, D_FF), BF16),
                            pltpu.VMEM((D_FF, D_MODEL), BF16)],
        ),
        out_shape=jax.ShapeDtypeStruct((n_rows, HALF), I32),
        compiler_params=_params("arbitrary"),
        name="moe_experts",
    )(blk_e, n_valid, xs, wg, bg, wu, bu, wd, bd)


def _sc_gather_kernel(table_hbm, idx_hbm, out_hbm, idx_v, rows_v, sem, *, rows_per_worker):
    base = _sc_worker() * rows_per_worker

    @pl.loop(0, rows_per_worker // SC_GATHER_ROWS)
    def _(j):
        off = base + j * SC_GATHER_ROWS
        pltpu.sync_copy(idx_hbm.at[pl.ds(off, SC_GATHER_ROWS)], idx_v)
        pltpu.async_copy(table_hbm.at[idx_v], rows_v, sem).wait()
        pltpu.sync_copy(rows_v, out_hbm.at[pl.ds(off, SC_GATHER_ROWS)])


def _sc_gather_rows(table, idx):
    n, width = idx.shape[0], table.shape[1]
    workers = V7X_SC_CORES * V7X_SC_SUBCORES
    assert n % (workers * SC_GATHER_ROWS) == 0
    return pl.kernel(
        functools.partial(_sc_gather_kernel, rows_per_worker=n // workers),
        out_type=jax.ShapeDtypeStruct((n, width), table.dtype),
        mesh=_sc_mesh(),
        scratch_types=[pltpu.VMEM((SC_GATHER_ROWS,), I32), pltpu.VMEM((SC_GATHER_ROWS, width), table.dtype),
                       pltpu.SemaphoreType.DMA],
        name="moe_gather_sc",
    )(table, idx)


def _ln2_kernel(y4_ref, h_ref, tw_ref, g2_ref, b2_ref, o_ref):
    tw = tw_ref[...]
    f_lo = jnp.zeros(y4_ref.shape[1:], F32)
    f_hi = jnp.zeros(y4_ref.shape[1:], F32)
    for k in range(TOP_K):
        lo, hi = _unpack_bf16_pair(y4_ref[k])
        w = tw[:, k:k + 1]
        f_lo = f_lo + w * lo
        f_hi = f_hi + w * hi
    f = jnp.concatenate([f_lo, f_hi], -1)
    o_ref[...] = _layer_norm(DEEPNORM_ALPHA * h_ref[...] + f, g2_ref[...], b2_ref[...])


def _weighted_sum_ln2(y4, h, top_w, ln_g, ln_b, tm):
    T = h.shape[0]
    tm = min(tm, T)
    full = lambda a: pl.BlockSpec(a.shape, lambda i: (0,) * a.ndim)
    return pl.pallas_call(
        _ln2_kernel,
        grid=(T // tm,),
        in_specs=[pl.BlockSpec((TOP_K, tm, HALF), lambda i: (0, i, 0)),
                  pl.BlockSpec((tm, D_MODEL), lambda i: (i, 0)),
                  pl.BlockSpec((tm, TOP_K), lambda i: (i, 0)),
                  full(ln_g), full(ln_b)],
        out_specs=pl.BlockSpec((tm, D_MODEL), lambda i: (i, 0)),
        out_shape=jax.ShapeDtypeStruct((T, D_MODEL), F32),
        compiler_params=_params("parallel"),
        name="moe_sum_ln2",
    )(y4, h, top_w, ln_g, ln_b)


def _rope_table(seq, dim):
    inv = ROPE_THETA ** (-jnp.arange(0, dim, 2, dtype=F32) / dim)
    ang = jnp.arange(seq, dtype=F32)[:, None] * inv[None, :]
    return jnp.cos(ang), jnp.sin(ang)


def _prepare_weights(w_in, ret_decay_logit, ret_gn_g, w_ret_o, mla_q_norm_g, w_uq, mla_kv_norm_g,
                     w_uk, w_uv, w_mla_o, w_out, ln1_g, ln1_b, router_w, router_b, exp_w_gate,
                     exp_b_gate, exp_w_up, exp_b_up, exp_w_down, exp_b_down, ln2_g, ln2_b):
    o = 0
    seg = {}
    for name, width in (("q", RET_QK_W), ("k", RET_QK_W), ("v", RET_V_W), ("g", RET_V_W),
                        ("cq", Q_LORA), ("ckv", KV_LORA), ("kpe", MLA_ROPE),
                        ("ga", D_MODEL), ("gb", D_MODEL)):
        seg[name] = w_in[:, o:o + width]
        o += width
    half = MLA_ROPE // 2
    swap = lambda w: jnp.concatenate([w[:, half:], w[:, :half]], -1)
    w_qk = jnp.concatenate([seg["q"], seg["k"]], -1).astype(BF16)
    w_vg = jnp.concatenate([seg["v"], seg["g"], seg["ga"], seg["gb"]], -1).astype(BF16)
    w_mla = jnp.concatenate([seg["cq"], seg["ckv"], seg["kpe"], swap(seg["kpe"])], -1).astype(BF16)
    wq = w_uq.reshape(Q_LORA, MLA_HEADS, MLA_NOPE + MLA_ROPE)
    wq_pe = wq[:, :, MLA_NOPE:]
    wq_ext = jnp.concatenate([wq[:, :, :MLA_NOPE], wq_pe,
                              jnp.concatenate([wq_pe[:, :, half:], wq_pe[:, :, :half]], -1)], -1)
    rw_hi = router_w.astype(BF16)
    return dict(
        w_qk=w_qk, w_vg=w_vg, w_mla=w_mla,
        log_gamma=jax.nn.log_sigmoid(ret_decay_logit.astype(F32)),
        gn_g=ret_gn_g.reshape(1, RET_V_W),
        w_ret_o=w_ret_o.astype(BF16),
        qn_g=mla_q_norm_g.reshape(1, Q_LORA), kvn_g=mla_kv_norm_g.reshape(1, KV_LORA),
        wq_ext=wq_ext.reshape(Q_LORA, MLA_HEADS * MLA_QK).astype(BF16),
        wk2=w_uk.reshape(KV_LORA, MLA_HEADS * MLA_NOPE).astype(BF16),
        wv2=w_uv.reshape(KV_LORA, MLA_HEADS * MLA_V).astype(BF16),
        w_mla_o=w_mla_o.astype(BF16), w_out=w_out.astype(BF16),
        ln1_g=ln1_g.reshape(1, D_MODEL), ln1_b=ln1_b.reshape(1, D_MODEL),
        rw_hi=rw_hi,
        rw_cat=jnp.concatenate([rw_hi, (router_w - rw_hi.astype(F32)).astype(BF16)], -1),
        rb=router_b.reshape(1, N_EXPERTS),
        wg=exp_w_gate, bg=exp_b_gate.reshape(N_EXPERTS, 1, D_FF),
        wu=exp_w_up, bu=exp_b_up.reshape(N_EXPERTS, 1, D_FF),
        wd=exp_w_down, bd=exp_b_down.reshape(N_EXPERTS, 1, D_MODEL),
        ln2_g=ln2_g.reshape(1, D_MODEL), ln2_b=ln2_b.reshape(1, D_MODEL),
    )


def _encoder_layer(x3, p):
    B, S, D = x3.shape
    T = B * S
    x = x3.reshape(T, D)

    cos_r, sin_r = _rope_table(S, RET_DK)
    cos_m, sin_m = _rope_table(S, MLA_ROPE)
    zeros = jnp.zeros((S, 128 - MLA_ROPE), F32)
    cc = jnp.concatenate([cos_m, cos_m, zeros], -1)
    ss = jnp.concatenate([-sin_m, sin_m, zeros], -1)

    qk = _qk_rope_proj(x, p["w_qk"], cos_r, sin_r, S, tm=2048)
    vg = _linear(x, p["w_vg"], BF16, tm=2048, tn=1024, name="proj_vg")
    mla = _linear(x, p["w_mla"], F32, tm=2048, tn=MLA_SEG, name="proj_mla")

    y_ret = _retention(p["log_gamma"], qk, vg, p["gn_g"], B, S, chunk=RET_CHUNK).reshape(T, RET_V_W)
    q, k, v = _mla_prep(mla, cc, ss, p["qn_g"], p["kvn_g"], p["wq_ext"], p["wk2"], p["wv2"], B, S, tm=512)
    o_mla = _attention(q, k, v, ATTN_SCORE_BYTES).reshape(T, MLA_HEADS * MLA_V)

    h, hp, top_e, top_w = _merge(x, y_ret, o_mla, vg, p["w_ret_o"], p["w_mla_o"], p["w_out"],
                                 p["ln1_g"], p["ln1_b"], p["rw_hi"], p["rw_cat"], p["rb"], tm=512)

    rank, counts = _rank(top_e, tm=512)
    tb = EXPERT_BLOCK
    n_rows = T * TOP_K + N_EXPERTS * tb
    counts = counts.reshape(N_EXPERTS)
    padded = (counts + tb - 1) // tb * tb
    pend = jnp.cumsum(padded)
    pstart = pend - padded
    dest = pstart[top_e] + rank
    blk_start = jnp.arange(n_rows // tb, dtype=I32) * tb
    blk_e = jnp.minimum(jnp.sum((pend[None, :] <= blk_start[:, None]).astype(I32), -1), N_EXPERTS - 1)
    n_valid = (pend[-1:] // tb).astype(I32)

    seg = jnp.concatenate([padded - counts, n_rows - pend[-1:]])
    seg_end = jnp.cumsum(seg)
    seg_base = jnp.concatenate([pstart + counts, pend[-1:]])
    j = jnp.arange(n_rows - T * TOP_K, dtype=I32)
    s = jnp.sum((seg_end[None, :] <= j[:, None]).astype(I32), -1)
    pad_rows = (seg_base[s] + j - (seg_end - seg)[s]).astype(I32)

    R = SC_GATHER_ROWS
    dest_chunks = dest.reshape(T // R, R, TOP_K).transpose(0, 2, 1)
    xs = _sc_dispatch(hp, dest_chunks, pad_rows.reshape(-1, R), n_rows)
    ys = _experts(blk_e, n_valid, xs, p["wg"], p["bg"], p["wu"], p["bu"], p["wd"], p["bd"], tb)
    y4 = _sc_gather_rows(ys, dest.T.reshape(T * TOP_K)).reshape(TOP_K, T, HALF)
    out = _weighted_sum_ln2(y4, h, top_w, p["ln2_g"], p["ln2_b"], tm=512)
    return out.reshape(B, S, D)


def kernel(x_prompt, x_sample, w_in, ret_decay_logit, ret_gn_g, w_ret_o, mla_q_norm_g, w_uq, mla_kv_norm_g, w_uk, w_uv, w_mla_o, w_out, ln1_g, ln1_b, router_w, router_b, exp_w_gate, exp_b_gate, exp_w_up, exp_b_up, exp_w_down, exp_b_down, ln2_g, ln2_b):
    params = (w_in, ret_decay_logit, ret_gn_g, w_ret_o, mla_q_norm_g, w_uq, mla_kv_norm_g, w_uk, w_uv,
              w_mla_o, w_out, ln1_g, ln1_b, router_w, router_b, exp_w_gate, exp_b_gate, exp_w_up,
              exp_b_up, exp_w_down, exp_b_down, ln2_g, ln2_b)
    depth = w_in.shape[0]
    y_prompt, y_sample = x_prompt, x_sample
    for l in range(depth):
        p = _prepare_weights(*[w[l] for w in params])
        y_prompt = _encoder_layer(y_prompt, p)
        y_sample = _encoder_layer(y_sample, p)
    return (y_prompt, y_sample)
```
